```python
import jax, jax.numpy as jnp
from jax import lax
import numpy as np

D_MODEL = 1024
BATCH = 1
SEQ = 16384
DEPTH = 1
DEC_BATCH = 4
DEC_SEQ = 8192
PAST_LEN = 128

HEAD_DIM = 64
N_Q_HEADS = 8
N_KV_HEADS = 2
ATTN_WIDTH = N_Q_HEADS * HEAD_DIM
KV_WIDTH = N_KV_HEADS * HEAD_DIM
WINDOW = 128
ATTN_BLOCK = 128
ROPE_THETA = 10000.0
HGRN_HEADS = 4
HGRN_DK = 128
HGRN_DV = 128
HGRN_WIDTH = HGRN_HEADS * HGRN_DK
HGRN_CHUNK = 64
D_FF = 2816
CONV_WIDTH = 3
NORM_EPS = 1e-6
D_IN = ATTN_WIDTH + 2 * KV_WIDTH + 5 * HGRN_WIDTH + 2 * D_MODEL

kernel_name = 'hybrid_bidir_swa_hgrn2_encoder'


def rms_norm(x, w):
    xf = x.astype(jnp.float32)
    y = xf * lax.rsqrt(jnp.mean(xf * xf, axis=-1, keepdims=True) + NORM_EPS)
    return (y * w.astype(jnp.float32)).astype(x.dtype)


def rotary(x):
    T = x.shape[1]
    half = HEAD_DIM // 2
    inv_freq = ROPE_THETA ** (-jnp.arange(half, dtype=jnp.float32) / half)
    ang = jnp.arange(T, dtype=jnp.float32)[:, None] * inv_freq[None, :]
    cos = jnp.cos(ang)[None, :, None, :]
    sin = jnp.sin(ang)[None, :, None, :]
    x1 = x[..., :half].astype(jnp.float32)
    x2 = x[..., half:].astype(jnp.float32)
    out = jnp.concatenate([x1 * cos - x2 * sin, x2 * cos + x1 * sin], axis=-1)
    return out.astype(x.dtype)


def window_attention(q, k, v, sink):
    B, T = q.shape[0], q.shape[1]
    C = ATTN_BLOCK
    nb = T // C
    R = N_Q_HEADS // N_KV_HEADS
    qb = q.reshape(B, nb, C, N_KV_HEADS, R, HEAD_DIM)

    def windows(t):
        tp = jnp.pad(t, ((0, 0), (C, C), (0, 0), (0, 0))).reshape(B, nb + 2, C, N_KV_HEADS, HEAD_DIM)
        return jnp.concatenate([tp[:, :-2], tp[:, 1:-1], tp[:, 2:]], axis=2)

    kw = windows(k)
    vw = windows(v)
    s = jnp.einsum('bnqgrd,bnkgd->bngrqk', qb, kw).astype(jnp.float32) * (HEAD_DIM ** -0.5)
    blk = jnp.arange(nb)[:, None, None]
    qpos = blk * C + jnp.arange(C)[None, :, None]
    kpos = (blk - 1) * C + jnp.arange(3 * C)[None, None, :]
    valid = (jnp.abs(qpos - kpos) <= WINDOW) & (kpos >= 0) & (kpos < T)
    s = jnp.where(valid[None, :, None, None], s, -jnp.inf)
    sink_l = sink.astype(jnp.float32).reshape(N_KV_HEADS, R)[None, None, :, :, None, None]
    m = jnp.maximum(jnp.max(s, axis=-1, keepdims=True), sink_l)
    p = jnp.exp(s - m)
    denom = jnp.sum(p, axis=-1, keepdims=True) + jnp.exp(sink_l - m)
    o = jnp.einsum('bngrqk,bnkgd->bnqgrd', (p / denom).astype(v.dtype), vw)
    return o.reshape(B, T, ATTN_WIDTH)


def gla_chunk_scan(q, k, v, g):
    B, T, H, DK = q.shape
    DV = v.shape[-1]
    C = HGRN_CHUNK
    n = T // C
    rs = lambda t: t.reshape(B, n, C, H, t.shape[-1])
    q, k, v, g = rs(q), rs(k), rs(v), rs(g)
    b = jnp.cumsum(g, axis=2)
    ref = b[:, :, C // 2 - 1:C // 2]
    b_last = b[:, :, -1:]
    A = jnp.einsum('bnihd,bnjhd->bnhij', q * jnp.exp(b - ref), k * jnp.exp(ref - b))
    lower = jnp.tril(jnp.ones((C, C), dtype=bool))
    A = jnp.where(lower, A, 0.0)
    o_intra = jnp.einsum('bnhij,bnjhv->bnihv', A, v)
    q_dec = q * jnp.exp(b)
    k_dec = k * jnp.exp(b_last - b)
    decay = jnp.exp(b_last[:, :, 0])

    def step(S, xs):
        qd, kd, vv, dc = xs
        o = jnp.einsum('bihd,bhdv->bihv', qd, S)
        S = dc[..., None] * S + jnp.einsum('bihd,bihv->bhdv', kd, vv)
        return S, o

    S0 = jnp.zeros((B, H, DK, DV), jnp.float32)
    xs = (jnp.moveaxis(q_dec, 1, 0), jnp.moveaxis(k_dec, 1, 0), jnp.moveaxis(v, 1, 0), jnp.moveaxis(decay, 1, 0))
    _, o_inter = lax.scan(step, S0, xs)
    o = o_intra + jnp.moveaxis(o_inter, 0, 1)
    return o.reshape(B, T, H, DV)


def hgrn2_bidir(xq, xf_fwd, xf_bwd, xi, xg, lb, gnorm_w):
    B, T = xq.shape[0], xq.shape[1]
    sh = lambda t: t.astype(jnp.float32).reshape(B, T, HGRN_HEADS, HGRN_DK)
    q = sh(xq)
    i = xi.astype(jnp.float32).reshape(B, T, HGRN_HEADS, HGRN_DV)

    def gates(xf, lb_d):
        lbh = lb_d.reshape(HGRN_HEADS, HGRN_DK)
        f = lbh + (1.0 - lbh) * jax.nn.sigmoid(sh(xf))
        return jnp.log(f), 1.0 - f

    g_f, k_f = gates(xf_fwd, lb[0])
    g_b, k_b = gates(xf_bwd, lb[1])
    o_f = gla_chunk_scan(q, k_f, i, g_f)
    flip = lambda t: jnp.flip(t, axis=1)
    o_b = flip(gla_chunk_scan(flip(q), flip(k_b), flip(i), flip(g_b)))
    o = o_f + o_b
    o = o * lax.rsqrt(jnp.mean(o * o, axis=-1, keepdims=True) + NORM_EPS)
    o = o.reshape(B, T, HGRN_WIDTH) * gnorm_w.astype(jnp.float32) * jax.nn.silu(xg.astype(jnp.float32))
    return o.astype(xq.dtype)


def conv_ffn(x, w_up, conv_w, conv_b, w_down):
    u = x @ w_up
    up = jnp.pad(u, ((0, 0), (1, 1), (0, 0)))
    u = conv_w[0] * up[:, :-2] + conv_w[1] * up[:, 1:-1] + conv_w[2] * up[:, 2:] + conv_b
    val, gate = jnp.split(u, 2, axis=-1)
    return (jax.nn.silu(gate) * val) @ w_down


def encoder_layer(x, lb, norm_mix_w, w_in, attn_sink, hgrn_norm_w, w_attn_out, w_hgrn_out,
                  w_mix_out, norm_ffn_w, w_up, conv_w, conv_b, w_down):
    B, T = x.shape[0], x.shape[1]
    h = rms_norm(x, norm_mix_w)
    z = h @ w_in
    sizes = [ATTN_WIDTH, KV_WIDTH, KV_WIDTH] + [HGRN_WIDTH] * 5 + [D_MODEL, D_MODEL]
    offs = [int(o) for o in np.cumsum(sizes)[:-1]]
    aq, ak, av, hq, hf_f, hf_b, hi, hg, gate_a, gate_h = jnp.split(z, offs, axis=-1)
    q = rotary(aq.reshape(B, T, N_Q_HEADS, HEAD_DIM))
    k = rotary(ak.reshape(B, T, N_KV_HEADS, HEAD_DIM))
    v = av.reshape(B, T, N_KV_HEADS, HEAD_DIM)
    a = window_attention(q, k, v, attn_sink)
    r = hgrn2_bidir(hq, hf_f, hf_b, hi, hg, lb, hgrn_norm_w)
    mixed = jax.nn.sigmoid(gate_a) * (a @ w_attn_out) + jax.nn.sigmoid(gate_h) * (r @ w_hgrn_out)
    x = x + mixed @ w_mix_out
    x = x + conv_ffn(rms_norm(x, norm_ffn_w), w_up, conv_w, conv_b, w_down)
    return x


def trunk(x, norm_mix_w, w_in, attn_sink, hgrn_lb_logits, hgrn_norm_w, w_attn_out, w_hgrn_out,
          w_mix_out, norm_ffn_w, w_up, conv_w, conv_b, w_down, norm_final_w):
    lb_all = jnp.cumsum(jax.nn.softmax(hgrn_lb_logits.astype(jnp.float32), axis=0), axis=0)
    for l in range(DEPTH):
        x = encoder_layer(x, lb_all[l], norm_mix_w[l], w_in[l], attn_sink[l], hgrn_norm_w[l],
                          w_attn_out[l], w_hgrn_out[l], w_mix_out[l], norm_ffn_w[l],
                          w_up[l], conv_w[l], conv_b[l], w_down[l])
    return rms_norm(x, norm_final_w)


def setup_inputs(seed: int = 0) -> dict:
    key = jax.random.key(seed)
    ks = jax.random.split(key, 16)
    nrm = lambda k, shape, s: jax.random.normal(k, shape, jnp.float32) * s
    return {
        'x_prompt': nrm(ks[0], (BATCH, SEQ, D_MODEL), 1.0),
        'x_sample': nrm(ks[1], (DEC_BATCH, DEC_SEQ, D_MODEL), 1.0),
        'norm_mix_w': 1.0 + nrm(ks[2], (DEPTH, D_MODEL), 0.02),
        'w_in': nrm(ks[3], (DEPTH, D_MODEL, D_IN), D_MODEL ** -0.5),
        'attn_sink': nrm(ks[4], (DEPTH, N_Q_HEADS), 0.5),
        'hgrn_lb_logits': nrm(ks[5], (DEPTH + 1, 2, HGRN_WIDTH), 0.5),
        'hgrn_norm_w': 1.0 + nrm(ks[6], (DEPTH, HGRN_WIDTH), 0.02),
        'w_attn_out': nrm(ks[7], (DEPTH, ATTN_WIDTH, D_MODEL), ATTN_WIDTH ** -0.5),
        'w_hgrn_out': nrm(ks[8], (DEPTH, HGRN_WIDTH, D_MODEL), HGRN_WIDTH ** -0.5),
        'w_mix_out': nrm(ks[9], (DEPTH, D_MODEL, D_MODEL), D_MODEL ** -0.5),
        'norm_ffn_w': 1.0 + nrm(ks[10], (DEPTH, D_MODEL), 0.02),
        'w_up': nrm(ks[11], (DEPTH, D_MODEL, 2 * D_FF), D_MODEL ** -0.5),
        'conv_w': nrm(ks[12], (DEPTH, CONV_WIDTH, 2 * D_FF), CONV_WIDTH ** -0.5),
        'conv_b': nrm(ks[13], (DEPTH, 2 * D_FF), 0.01),
        'w_down': nrm(ks[14], (DEPTH, D_FF, D_MODEL), D_FF ** -0.5),
        'norm_final_w': 1.0 + nrm(ks[15], (D_MODEL,), 0.02),
    }


def reference(x_prompt, x_sample, norm_mix_w, w_in, attn_sink, hgrn_lb_logits, hgrn_norm_w,
              w_attn_out, w_hgrn_out, w_mix_out, norm_ffn_w, w_up, conv_w, conv_b, w_down,
              norm_final_w):
    y_prompt = trunk(x_prompt, norm_mix_w, w_in, attn_sink, hgrn_lb_logits, hgrn_norm_w, w_attn_out,
                     w_hgrn_out, w_mix_out, norm_ffn_w, w_up, conv_w, conv_b, w_down, norm_final_w)
    y_sample = trunk(x_sample, norm_mix_w, w_in, attn_sink, hgrn_lb_logits, hgrn_norm_w, w_attn_out,
                     w_hgrn_out, w_mix_out, norm_ffn_w, w_up, conv_w, conv_b, w_down, norm_final_w)
    return (y_prompt, y_sample)
```

```python
import functools

import jax
import jax.numpy as jnp
import numpy as np
from jax import lax
from jax.experimental import pallas as pl
from jax.experimental.pallas import tpu as pltpu

D_MODEL = 1024
HEAD_DIM = 64
N_Q_HEADS = 8
N_KV_HEADS = 2
ATTN_WIDTH = N_Q_HEADS * HEAD_DIM
KV_WIDTH = N_KV_HEADS * HEAD_DIM
WINDOW = 128
ROPE_THETA = 10000.0
HGRN_HEADS = 4
HGRN_DK = 128
HGRN_WIDTH = HGRN_HEADS * HGRN_DK
HGRN_CHUNK = 64
D_FF = 2816
NORM_EPS = 1e-6

_OFF_Q = 0
_OFF_K = _OFF_Q + ATTN_WIDTH
_OFF_V = _OFF_K + KV_WIDTH
_OFF_HQ = _OFF_V + KV_WIDTH
_OFF_HFF = _OFF_HQ + HGRN_WIDTH
_OFF_HFB = _OFF_HFF + HGRN_WIDTH
_OFF_HI = _OFF_HFB + HGRN_WIDTH
_OFF_HG = _OFF_HI + HGRN_WIDTH
_OFF_GA = _OFF_HG + HGRN_WIDTH
_OFF_GH = _OFF_GA + D_MODEL
D_IN = _OFF_GH + D_MODEL

LANES = 128
SUBLANES = 8
VMEM_LIMIT_BYTES = 56 * 1024 * 1024

TOKEN_TILE = 512
ATTN_TILE = 512
FFN_COL_CHUNK = 256

BF16 = jnp.bfloat16
F32 = jnp.float32


def _const_spec(shape):
    nd = len(shape)
    return pl.BlockSpec(shape, lambda *_: (0,) * nd, pipeline_mode=pl.Buffered(1))


def _rms(x, w):
    ms = jnp.mean(x * x, axis=-1, keepdims=True)
    return x * lax.rsqrt(ms + NORM_EPS) * w


def _rotate_half_pairs(x):
    lane = lax.broadcasted_iota(jnp.int32, x.shape, 1)
    first_half = (lane % HEAD_DIM) < (HEAD_DIM // 2)
    fwd = pltpu.roll(x, LANES - HEAD_DIM // 2, 1)
    bwd = pltpu.roll(x, HEAD_DIM // 2, 1)
    return jnp.where(first_half, fwd, bwd)


def _cumsum_rows(x, reverse):
    n = x.shape[0]
    row = lax.broadcasted_iota(jnp.int32, x.shape, 0)
    s = 1
    while s < n:
        if reverse:
            x = x + jnp.where(row < n - s, pltpu.roll(x, n - s, 0), 0.0)
        else:
            x = x + jnp.where(row >= s, pltpu.roll(x, s, 0), 0.0)
        s *= 2
    return x


def _inproj_kernel(x_ref, nw_ref, w_ref, cos_ref, sin_ref, lbl_ref,
                   q_ref, k_ref, v_ref, hi_ref, hg_ref, ga_ref, gh_ref,
                   qd_f_ref, kd_f_ref, qx_f_ref, kx_f_ref, dc_f_ref,
                   qd_b_ref, kd_b_ref, qx_b_ref, kx_b_ref, dc_b_ref,
                   zh_ref):
    tm = x_ref.shape[0]
    h = _rms(x_ref[...], nw_ref[...]).astype(BF16)

    def proj(lo, hi):
        return jnp.dot(h, w_ref[:, lo:hi], preferred_element_type=F32)

    cos = cos_ref[...]
    sin = sin_ref[...]

    def rope(z):
        return z * cos + _rotate_half_pairs(z) * sin

    zq = proj(_OFF_Q, _OFF_K)
    for j in range(ATTN_WIDTH // LANES):
        sl = slice(j * LANES, (j + 1) * LANES)
        q_ref[:, sl] = (rope(zq[:, sl]) * (HEAD_DIM ** -0.5)).astype(BF16)
    k_ref[...] = rope(proj(_OFF_K, _OFF_V)).astype(BF16)
    v_ref[...] = proj(_OFF_V, _OFF_HQ).astype(BF16)
    hi_ref[...] = proj(_OFF_HI, _OFF_HG).astype(BF16)
    hg_ref[...] = proj(_OFF_HG, _OFF_GA).astype(BF16)
    ga_ref[...] = proj(_OFF_GA, _OFF_GH).astype(BF16)
    gh_ref[...] = proj(_OFF_GH, D_IN).astype(BF16)

    zh_ref[...] = proj(_OFF_HQ, _OFF_HI)

    lbl = lbl_ref[...]
    e = jnp.exp(lbl - jnp.max(lbl, axis=0, keepdims=True))
    lb = e[0:1, :] / jnp.sum(e, axis=0, keepdims=True)

    c = HGRN_CHUNK
    dirs = ((qd_f_ref, kd_f_ref, qx_f_ref, kx_f_ref, dc_f_ref),
            (qd_b_ref, kd_b_ref, qx_b_ref, kx_b_ref, dc_b_ref))

    def chunk_body(ci, carry):
        r0 = pl.multiple_of(ci * c, c)
        rows = pl.ds(r0, c)
        hq = zh_ref[rows, 0:HGRN_WIDTH]
        for d, (qd_ref, kd_ref, qx_ref, kx_ref, dc_ref) in enumerate(dirs):
            xf = zh_ref[rows, (1 + d) * HGRN_WIDTH:(2 + d) * HGRN_WIDTH]
            lbd = lb[:, d * HGRN_WIDTH:(d + 1) * HGRN_WIDTH]
            f = lbd + (1.0 - lbd) * jax.nn.sigmoid(xf)
            g = jnp.log(f)
            kk = 1.0 - f
            b = _cumsum_rows(g, reverse=(d == 1))
            if d == 0:
                ref = b[c // 2 - 1:c // 2, :]
                b_last = b[c - 1:c, :]
            else:
                ref = b[c // 2:c // 2 + 1, :]
                b_last = b[0:1, :]
            qd_ref[rows, :] = (hq * jnp.exp(b - ref)).astype(BF16)
            kd_ref[rows, :] = (kk * jnp.exp(ref - b)).astype(BF16)
            qx_ref[rows, :] = (hq * jnp.exp(b)).astype(BF16)
            kx_ref[rows, :] = (kk * jnp.exp(b_last - b)).astype(BF16)
            dc_ref[pl.ds(ci, 1), :] = jnp.exp(b_last)
        return carry

    lax.fori_loop(0, tm // c, chunk_body, 0)


def _inproj(x2, nw, w_in, cos, sin, lbl, seq_len):
    n = x2.shape[0]
    tm = TOKEN_TILE
    tiles_per_seq = seq_len // tm
    cpt = tm // HGRN_CHUNK
    row = lambda i: (i, 0)
    pos = lambda i: (i % tiles_per_seq, 0)
    bf = lambda w: jax.ShapeDtypeStruct((n, w), BF16)
    dec = jax.ShapeDtypeStruct((n // HGRN_CHUNK, HGRN_WIDTH), F32)
    tok = lambda w: pl.BlockSpec((tm, w), row)
    dspec = pl.BlockSpec((cpt, HGRN_WIDTH), row)
    hg4 = [bf(HGRN_WIDTH)] * 4
    hs4 = [tok(HGRN_WIDTH)] * 4
    return pl.pallas_call(
        _inproj_kernel,
        grid=(n // tm,),
        in_specs=[tok(D_MODEL), _const_spec((1, D_MODEL)), _const_spec((D_MODEL, D_IN)),
                  pl.BlockSpec((tm, LANES), pos), pl.BlockSpec((tm, LANES), pos),
                  _const_spec(lbl.shape)],
        out_specs=[tok(ATTN_WIDTH), tok(KV_WIDTH), tok(KV_WIDTH), tok(HGRN_WIDTH), tok(HGRN_WIDTH),
                   tok(D_MODEL), tok(D_MODEL)] + hs4 + [dspec] + hs4 + [dspec],
        out_shape=[bf(ATTN_WIDTH), bf(KV_WIDTH), bf(KV_WIDTH), bf(HGRN_WIDTH), bf(HGRN_WIDTH),
                   bf(D_MODEL), bf(D_MODEL)] + hg4 + [dec] + hg4 + [dec],
        scratch_shapes=[pltpu.VMEM((tm, 3 * HGRN_WIDTH), F32)],
        compiler_params=pltpu.CompilerParams(
            dimension_semantics=("arbitrary",), vmem_limit_bytes=VMEM_LIMIT_BYTES),
        name="inproj",
    )(x2, nw, w_in, cos, sin, lbl)


def _attn_kernel(sink_ref, q_ref, kp_ref, kc_ref, kn_ref, vp_ref, vc_ref, vn_ref, o_ref,
                 kw_ref, vw_ref):
    i = pl.program_id(1)
    nblk = pl.num_programs(1)
    tq = q_ref.shape[0]
    w = WINDOW
    kw_ref[0:w, :] = kp_ref[...]
    kw_ref[w:w + tq, :] = kc_ref[...]
    kw_ref[w + tq:, :] = kn_ref[...]
    vw_ref[0:w, :] = vp_ref[...]
    vw_ref[w:w + tq, :] = vc_ref[...]
    vw_ref[w + tq:, :] = vn_ref[...]

    lane = lax.broadcasted_iota(jnp.int32, (w, LANES), 1)
    lo_half = lane < HEAD_DIM
    lane_v = lax.broadcasted_iota(jnp.int32, (3 * w, LANES), 1)
    lo_half_v = lane_v < HEAD_DIM
    r = lax.broadcasted_iota(jnp.int32, (w, 3 * w), 0)
    cc = lax.broadcasted_iota(jnp.int32, (w, 3 * w), 1)
    band = jnp.abs(r + w - cc) <= w
    n_pairs = N_Q_HEADS // 2

    for s in range(tq // w):
        valid = band
        if s == 0:
            valid = jnp.logical_and(valid, jnp.logical_or(cc >= w, i != 0))
        if s == tq // w - 1:
            valid = jnp.logical_and(valid, jnp.logical_or(cc < 2 * w, i != nblk - 1))
        kwin = kw_ref[s * w:s * w + 3 * w, :]
        vwin = vw_ref[s * w:s * w + 3 * w, :]
        zero_v = jnp.zeros_like(vwin)
        vblk = jnp.concatenate([jnp.where(lo_half_v, vwin, zero_v),
                                jnp.where(lo_half_v, zero_v, vwin)], axis=0)
        parts = []
        for j in range(n_pairs):
            qj = q_ref[s * w:(s + 1) * w, j * LANES:(j + 1) * LANES]
            zero_q = jnp.zeros_like(qj)
            parts.append(jnp.where(lo_half, qj, zero_q))
            parts.append(jnp.where(lo_half, zero_q, qj))
        lhs = jnp.concatenate(parts, axis=0)
        sc = lax.dot_general(lhs, kwin, (((1,), (1,)), ((), ())),
                             preferred_element_type=F32)
        for j in range(n_pairs):
            pn = []
            for t in range(2):
                head = j + t * n_pairs
                blk = 2 * j + t
                sb = jnp.where(valid, sc[blk * w:(blk + 1) * w, :], -jnp.inf)
                sink = sink_ref[head]
                m = jnp.maximum(jnp.max(sb, axis=-1, keepdims=True), sink)
                p = jnp.exp(sb - m)
                denom = jnp.sum(p, axis=-1, keepdims=True) + jnp.exp(sink - m)
                pn.append((p * (1.0 / denom)).astype(BF16))
            pcat = jnp.concatenate(pn, axis=1)
            o = jnp.dot(pcat, vblk, preferred_element_type=F32)
            o_ref[s * w:(s + 1) * w, j * LANES:(j + 1) * LANES] = o.astype(BF16)


def _attn(sink, q, k, v, batch, seq_len):
    tq = ATTN_TILE
    w = WINDOW
    nblk = seq_len // tq
    sub = tq // w
    nrow_blocks = batch * seq_len // w
    cur = lambda b, i, *_: (b * nblk + i, 0)
    prev = lambda b, i, *_: (jnp.maximum((b * nblk + i) * sub - 1, 0), 0)
    nxt = lambda b, i, *_: (jnp.minimum((b * nblk + i + 1) * sub, nrow_blocks - 1), 0)
    kv_cur = pl.BlockSpec((tq, KV_WIDTH), cur)
    kv_prev = pl.BlockSpec((w, KV_WIDTH), prev)
    kv_next = pl.BlockSpec((w, KV_WIDTH), nxt)
    return pl.pallas_call(
        _attn_kernel,
        grid_spec=pltpu.PrefetchScalarGridSpec(
            num_scalar_prefetch=1,
            grid=(batch, nblk),
            in_specs=[pl.BlockSpec((tq, ATTN_WIDTH), cur),
                      kv_prev, kv_cur, kv_next, kv_prev, kv_cur, kv_next],
            out_specs=pl.BlockSpec((tq, ATTN_WIDTH), cur),
            scratch_shapes=[pltpu.VMEM((tq + 2 * w, KV_WIDTH), BF16),
                            pltpu.VMEM((tq + 2 * w, KV_WIDTH), BF16)],
        ),
        out_shape=jax.ShapeDtypeStruct(q.shape, BF16),
        compiler_params=pltpu.CompilerParams(
            dimension_semantics=("arbitrary", "arbitrary"), vmem_limit_bytes=VMEM_LIMIT_BYTES),
        name="attn",
    )(sink, q, k, k, k, v, v, v)


def _hgrn_kernel(qd_f_ref, kd_f_ref, qx_f_ref, kx_f_ref, dc_f_ref, v_f_ref,
                 qd_b_ref, kd_b_ref, qx_b_ref, kx_b_ref, dc_b_ref, v_b_ref,
                 of_ref, ob_ref, s_ref):
    @pl.when(pl.program_id(1) == 0)
    def _():
        s_ref[...] = jnp.zeros_like(s_ref)

    c = HGRN_CHUNK
    n_chunks = of_ref.shape[0] // c
    r = lax.broadcasted_iota(jnp.int32, (c, c), 0)
    cc = lax.broadcasted_iota(jnp.int32, (c, c), 1)
    keep = (r >= cc, r <= cc)
    dirs = ((qd_f_ref, kd_f_ref, qx_f_ref, kx_f_ref, dc_f_ref, v_f_ref, of_ref),
            (qd_b_ref, kd_b_ref, qx_b_ref, kx_b_ref, dc_b_ref, v_b_ref, ob_ref))
    nt = (((1,), (1,)), ((), ()))
    tn = (((0,), (0,)), ((), ()))

    def step(t, carry):
        for d, (qd_ref, kd_ref, qx_ref, kx_ref, dc_ref, v_ref, o_ref) in enumerate(dirs):
            ci = t if d == 0 else n_chunks - 1 - t
            rows = pl.ds(pl.multiple_of(ci * c, c), c)
            dc = dc_ref[pl.ds(ci, 1), :]
            for hd in range(HGRN_HEADS):
                cols = slice(hd * HGRN_DK, (hd + 1) * HGRN_DK)
                idx = d * HGRN_HEADS + hd
                vh = v_ref[rows, cols]
                a = lax.dot_general(qd_ref[rows, cols], kd_ref[rows, cols], nt,
                                    preferred_element_type=F32)
                a = jnp.where(keep[d], a, 0.0).astype(BF16)
                st = s_ref[idx]
                o = jnp.dot(a, vh, preferred_element_type=F32)
                o = o + lax.dot_general(qx_ref[rows, cols], st.astype(BF16), nt,
                                        preferred_element_type=F32)
                o_ref[rows, cols] = o
                s_ref[idx] = st * dc[:, cols] + lax.dot_general(
                    vh, kx_ref[rows, cols], tn, preferred_element_type=F32)
        return carry

    lax.fori_loop(0, n_chunks, step, 0)


def _hgrn(prep_f, prep_b, v, batch, seq_len):
    tm = TOKEN_TILE
    nblk = seq_len // tm
    cpt = tm // HGRN_CHUNK
    fwd = lambda b, i: (b * nblk + i, 0)
    bwd = lambda b, i: (b * nblk + nblk - 1 - i, 0)
    n = v.shape[0]

    def specs(imap):
        t = pl.BlockSpec((tm, HGRN_WIDTH), imap)
        return [t, t, t, t, pl.BlockSpec((cpt, HGRN_WIDTH), imap), t]

    out = jax.ShapeDtypeStruct((n, HGRN_WIDTH), F32)
    return pl.pallas_call(
        _hgrn_kernel,
        grid=(batch, nblk),
        in_specs=specs(fwd) + specs(bwd),
        out_specs=[pl.BlockSpec((tm, HGRN_WIDTH), fwd), pl.BlockSpec((tm, HGRN_WIDTH), bwd)],
        out_shape=[out, out],
        scratch_shapes=[pltpu.VMEM((2 * HGRN_HEADS, HGRN_DK, HGRN_DK), F32)],
        compiler_params=pltpu.CompilerParams(
            dimension_semantics=("arbitrary", "arbitrary"), vmem_limit_bytes=VMEM_LIMIT_BYTES),
        name="hgrn",
    )(*prep_f, v, *prep_b, v)


def _mix_kernel(x_ref, a_ref, of_ref, ob_ref, hg_ref, ga_ref, gh_ref,
                gw_ref, wa_ref, wh_ref, wm_ref, x1_ref):
    o = of_ref[...] + ob_ref[...]
    hg = hg_ref[...].astype(F32)
    gw = gw_ref[...]
    parts = []
    for hd in range(HGRN_HEADS):
        cols = slice(hd * HGRN_DK, (hd + 1) * HGRN_DK)
        oh = o[:, cols]
        ms = jnp.mean(oh * oh, axis=-1, keepdims=True)
        parts.append(oh * lax.rsqrt(ms + NORM_EPS))
    on = jnp.concatenate(parts, axis=1)
    rr = (on * gw * (hg * jax.nn.sigmoid(hg))).astype(BF16)
    pa = jnp.dot(a_ref[...], wa_ref[...], preferred_element_type=F32)
    ph = jnp.dot(rr, wh_ref[...], preferred_element_type=F32)
    mixed = (jax.nn.sigmoid(ga_ref[...].astype(F32)) * pa
             + jax.nn.sigmoid(gh_ref[...].astype(F32)) * ph).astype(BF16)
    x1_ref[...] = x_ref[...] + jnp.dot(mixed, wm_ref[...], preferred_element_type=F32)


def _mix(x2, a, of, ob, hg, ga, gh, gw, wa, wh, wm):
    n = x2.shape[0]
    tm = TOKEN_TILE
    row = lambda i: (i, 0)
    tok = lambda w: pl.BlockSpec((tm, w), row)
    return pl.pallas_call(
        _mix_kernel,
        grid=(n // tm,),
        in_specs=[tok(D_MODEL), tok(ATTN_WIDTH), tok(HGRN_WIDTH), tok(HGRN_WIDTH), tok(HGRN_WIDTH),
                  tok(D_MODEL), tok(D_MODEL),
                  _const_spec((1, HGRN_WIDTH)), _const_spec((ATTN_WIDTH, D_MODEL)),
                  _const_spec((HGRN_WIDTH, D_MODEL)), _const_spec((D_MODEL, D_MODEL))],
        out_specs=tok(D_MODEL),
        out_shape=jax.ShapeDtypeStruct((n, D_MODEL), F32),
        compiler_params=pltpu.CompilerParams(
            dimension_semantics=("arbitrary",), vmem_limit_bytes=VMEM_LIMIT_BYTES),
        name="mix",
    )(x2, a, of, ob, hg, ga, gh, gw, wa, wh, wm)


def _ffn_kernel(xp_ref, xc_ref, xn_ref, nw_ref, wu_ref, cw_ref, cb_ref, wd_ref, fw_ref,
                y_ref, hb_ref, *, tiles_per_seq):
    i = pl.program_id(0)
    tm = xc_ref.shape[0]
    hl = SUBLANES
    nw = nw_ref[...]
    pos = i % tiles_per_seq
    keep_prev = jnp.where(pos != 0, 1.0, 0.0)
    keep_next = jnp.where(pos != tiles_per_seq - 1, 1.0, 0.0)
    x1 = xc_ref[...]
    hb_ref[0:hl, :] = _rms(xp_ref[...], nw) * keep_prev
    hb_ref[hl:hl + tm, :] = _rms(x1, nw)
    hb_ref[hl + tm:, :] = _rms(xn_ref[...], nw) * keep_next
    h = hb_ref[...].astype(BF16)
    rows = tm + 2 * hl

    cw = cw_ref[...]
    cb = cb_ref[...]

    def conv(u, lo, hi):
        up = pltpu.roll(u, 1, 0)
        un = pltpu.roll(u, rows - 1, 0)
        y = cw[0:1, lo:hi] * up + cw[1:2, lo:hi] * u + cw[2:3, lo:hi] * un + cb[:, lo:hi]
        return y[hl:hl + tm, :]

    acc = x1
    ck = FFN_COL_CHUNK
    for j in range(D_FF // ck):
        lo, hi = j * ck, (j + 1) * ck
        uv = jnp.dot(h, wu_ref[:, lo:hi], preferred_element_type=F32)
        ug = jnp.dot(h, wu_ref[:, D_FF + lo:D_FF + hi], preferred_element_type=F32)
        val = conv(uv, lo, hi)
        gate = conv(ug, D_FF + lo, D_FF + hi)
        act = (gate * jax.nn.sigmoid(gate) * val).astype(BF16)
        acc = acc + jnp.dot(act, wd_ref[lo:hi, :], preferred_element_type=F32)
    y_ref[...] = _rms(acc, fw_ref[...])


def _ffn(x1, nw, wu, cw, cb, wd, fw, seq_len):
    n = x1.shape[0]
    tm = TOKEN_TILE
    hl = SUBLANES
    per = tm // hl
    nhalo = n // hl
    row = lambda i: (i, 0)
    prev = lambda i: (jnp.maximum(i * per - 1, 0), 0)
    nxt = lambda i: (jnp.minimum((i + 1) * per, nhalo - 1), 0)
    kern = functools.partial(_ffn_kernel, tiles_per_seq=seq_len // tm)
    return pl.pallas_call(
        kern,
        grid=(n // tm,),
        in_specs=[pl.BlockSpec((hl, D_MODEL), prev), pl.BlockSpec((tm, D_MODEL), row),
                  pl.BlockSpec((hl, D_MODEL), nxt),
                  _const_spec((1, D_MODEL)), _const_spec((D_MODEL, 2 * D_FF)),
                  _const_spec((3, 2 * D_FF)), _const_spec((1, 2 * D_FF)),
                  _const_spec((D_FF, D_MODEL)), _const_spec((1, D_MODEL))],
        out_specs=pl.BlockSpec((tm, D_MODEL), row),
        out_shape=jax.ShapeDtypeStruct((n, D_MODEL), F32),
        scratch_shapes=[pltpu.VMEM((tm + 2 * hl, D_MODEL), F32)],
        compiler_params=pltpu.CompilerParams(
            dimension_semantics=("arbitrary",), vmem_limit_bytes=VMEM_LIMIT_BYTES),
        name="ffn",
    )(x1, x1, x1, nw, wu, cw, cb, wd, fw)


def _head_pair_perm():
    half = N_Q_HEADS // 2
    order = []
    for j in range(half):
        order += [j, j + half]
    return np.concatenate([np.arange(h * HEAD_DIM, (h + 1) * HEAD_DIM) for h in order])


def _rope_tables(seq_len):
    half = HEAD_DIM // 2
    inv_freq = ROPE_THETA ** (-jnp.arange(half, dtype=F32) / half)
    ang = jnp.arange(seq_len, dtype=F32)[:, None] * inv_freq[None, :]
    cos = jnp.cos(ang)
    sin = jnp.sin(ang)
    reps = LANES // HEAD_DIM
    cos_t = jnp.tile(jnp.concatenate([cos, cos], axis=1), (1, reps))
    sin_t = jnp.tile(jnp.concatenate([-sin, sin], axis=1), (1, reps))
    return cos_t, sin_t


def _trunk(x, p):
    batch, seq_len, _ = x.shape
    n = batch * seq_len
    x2 = x.reshape(n, D_MODEL)
    cos, sin = _rope_tables(seq_len)
    (q, k, v, hi, hg, ga, gh,
     qd_f, kd_f, qx_f, kx_f, dc_f,
     qd_b, kd_b, qx_b, kx_b, dc_b) = _inproj(x2, p["norm_mix_w"], p["w_in"], cos, sin, p["lbl"], seq_len)
    a = _attn(p["sink"], q, k, v, batch, seq_len)
    of, ob = _hgrn((qd_f, kd_f, qx_f, kx_f, dc_f), (qd_b, kd_b, qx_b, kx_b, dc_b), hi, batch, seq_len)
    x1 = _mix(x2, a, of, ob, hg, ga, gh, p["hgrn_norm_w"], p["w_attn_out"], p["w_hgrn_out"], p["w_mix_out"])
    y = _ffn(x1, p["norm_ffn_w"], p["w_up"], p["conv_w"], p["conv_b"], p["w_down"], p["norm_final_w"], seq_len)
    return y.reshape(batch, seq_len, D_MODEL)


def kernel(x_prompt, x_sample, norm_mix_w, w_in, attn_sink, hgrn_lb_logits, hgrn_norm_w, w_attn_out, w_hgrn_out, w_mix_out, norm_ffn_w, w_up, conv_w, conv_b, w_down, norm_final_w):
    perm = _head_pair_perm()
    w_in0 = w_in[0]
    w_in_p = jnp.concatenate([w_in0[:, :ATTN_WIDTH][:, perm], w_in0[:, ATTN_WIDTH:]], axis=1)
    p = {
        "norm_mix_w": norm_mix_w[0].reshape(1, D_MODEL),
        "w_in": w_in_p.astype(BF16),
        "sink": attn_sink[0].astype(F32),
        "lbl": hgrn_lb_logits.astype(F32).reshape(hgrn_lb_logits.shape[0], 2 * HGRN_WIDTH),
        "hgrn_norm_w": hgrn_norm_w[0].astype(F32).reshape(1, HGRN_WIDTH),
        "w_attn_out": w_attn_out[0][perm, :].astype(BF16),
        "w_hgrn_out": w_hgrn_out[0].astype(BF16),
        "w_mix_out": w_mix_out[0].astype(BF16),
        "norm_ffn_w": norm_ffn_w[0].reshape(1, D_MODEL),
        "w_up": w_up[0].astype(BF16),
        "conv_w": conv_w[0],
        "conv_b": conv_b[0].reshape(1, 2 * D_FF),
        "w_down": w_down[0].astype(BF16),
        "norm_final_w": norm_final_w.reshape(1, D_MODEL),
    }
    return (_trunk(x_prompt, p), _trunk(x_sample, p))
```

```python
import functools

import jax
import jax.numpy as jnp
import numpy as np
from jax import lax
from jax.experimental import pallas as pl
from jax.experimental.pallas import tpu as pltpu

D_MODEL = 1024
HEAD_DIM = 64
N_Q_HEADS = 8
N_KV_HEADS = 2
ATTN_WIDTH = N_Q_HEADS * HEAD_DIM
KV_WIDTH = N_KV_HEADS * HEAD_DIM
WINDOW = 128
ROPE_THETA = 10000.0
HGRN_HEADS = 4
HGRN_DK = 128
HGRN_WIDTH = HGRN_HEADS * HGRN_DK
HGRN_CHUNK = 64
D_FF = 2816
NORM_EPS = 1e-6

_OFF_Q = 0
_OFF_K = _OFF_Q + ATTN_WIDTH
_OFF_V = _OFF_K + KV_WIDTH
_OFF_HQ = _OFF_V + KV_WIDTH
_OFF_HFF = _OFF_HQ + HGRN_WIDTH
_OFF_HFB = _OFF_HFF + HGRN_WIDTH
_OFF_HI = _OFF_HFB + HGRN_WIDTH
_OFF_HG = _OFF_HI + HGRN_WIDTH
_OFF_GA = _OFF_HG + HGRN_WIDTH
_OFF_GH = _OFF_GA + D_MODEL
D_IN = _OFF_GH + D_MODEL

LANES = 128
SUBLANES = 8
VMEM_LIMIT_BYTES = 56 * 1024 * 1024

TOKEN_TILE = 512
ATTN_TILE = 512
FFN_COL_CHUNK = 256

BF16 = jnp.bfloat16
F32 = jnp.float32


def _const_spec(shape):
    nd = len(shape)
    return pl.BlockSpec(shape, lambda *_: (0,) * nd, pipeline_mode=pl.Buffered(1))


def _rms(x, w):
    ms = jnp.mean(x * x, axis=-1, keepdims=True)
    return x * lax.rsqrt(ms + NORM_EPS) * w


def _rotate_half_pairs(x):
    lane = lax.broadcasted_iota(jnp.int32, x.shape, 1)
    first_half = (lane % HEAD_DIM) < (HEAD_DIM // 2)
    fwd = pltpu.roll(x, LANES - HEAD_DIM // 2, 1)
    bwd = pltpu.roll(x, HEAD_DIM // 2, 1)
    return jnp.where(first_half, fwd, bwd)


def _cumsum_rows(x, reverse):
    n = x.shape[0]
    row = lax.broadcasted_iota(jnp.int32, x.shape, 0)
    s = 1
    while s < n:
        if reverse:
            x = x + jnp.where(row < n - s, pltpu.roll(x, n - s, 0), 0.0)
        else:
            x = x + jnp.where(row >= s, pltpu.roll(x, s, 0), 0.0)
        s *= 2
    return x


def _inproj_kernel(x_ref, nw_ref, w_ref, cos_ref, sin_ref, lbl_ref,
                   q_ref, k_ref, v_ref, hi_ref, hg_ref, ga_ref, gh_ref,
                   qd_f_ref, kd_f_ref, qx_f_ref, kx_f_ref, dc_f_ref,
                   qd_b_ref, kd_b_ref, qx_b_ref, kx_b_ref, dc_b_ref,
                   zh_ref):
    tm = x_ref.shape[0]
    h = _rms(x_ref[...], nw_ref[...]).astype(BF16)

    def proj(lo, hi):
        return jnp.dot(h, w_ref[:, lo:hi], preferred_element_type=F32)

    cos = cos_ref[...]
    sin = sin_ref[...]

    def rope(z):
        return z * cos + _rotate_half_pairs(z) * sin

    zq = proj(_OFF_Q, _OFF_K)
    for j in range(ATTN_WIDTH // LANES):
        sl = slice(j * LANES, (j + 1) * LANES)
        q_ref[:, sl] = (rope(zq[:, sl]) * (HEAD_DIM ** -0.5)).astype(BF16)
    k_ref[...] = rope(proj(_OFF_K, _OFF_V)).astype(BF16)
    v_ref[...] = proj(_OFF_V, _OFF_HQ).astype(BF16)
    hi_ref[...] = proj(_OFF_HI, _OFF_HG).astype(BF16)
    hg_ref[...] = proj(_OFF_HG, _OFF_GA).astype(BF16)
    ga_ref[...] = proj(_OFF_GA, _OFF_GH).astype(BF16)
    gh_ref[...] = proj(_OFF_GH, D_IN).astype(BF16)

    zh_ref[...] = proj(_OFF_HQ, _OFF_HI)

    lbl = lbl_ref[...]
    e = jnp.exp(lbl - jnp.max(lbl, axis=0, keepdims=True))
    lb = e[0:1, :] / jnp.sum(e, axis=0, keepdims=True)

    c = HGRN_CHUNK
    dirs = ((qd_f_ref, kd_f_ref, qx_f_ref, kx_f_ref, dc_f_ref),
            (qd_b_ref, kd_b_ref, qx_b_ref, kx_b_ref, dc_b_ref))

    def chunk_body(ci, carry):
        r0 = pl.multiple_of(ci * c, c)
        rows = pl.ds(r0, c)
        hq = zh_ref[rows, 0:HGRN_WIDTH]
        for d, (qd_ref, kd_ref, qx_ref, kx_ref, dc_ref) in enumerate(dirs):
            xf = zh_ref[rows, (1 + d) * HGRN_WIDTH:(2 + d) * HGRN_WIDTH]
            lbd = lb[:, d * HGRN_WIDTH:(d + 1) * HGRN_WIDTH]
            f = lbd + (1.0 - lbd) * jax.nn.sigmoid(xf)
            g = jnp.log(f)
            kk = 1.0 - f
            b = _cumsum_rows(g, reverse=(d == 1))
            if d == 0:
                ref = b[c // 2 - 1:c // 2, :]
                b_last = b[c - 1:c, :]
            else:
                ref = b[c // 2:c // 2 + 1, :]
                b_last = b[0:1, :]
            qd_ref[rows, :] = (hq * jnp.exp(b - ref)).astype(BF16)
            kd_ref[rows, :] = (kk * jnp.exp(ref - b)).astype(BF16)
            qx_ref[rows, :] = (hq * jnp.exp(b)).astype(BF16)
            kx_ref[rows, :] = (kk * jnp.exp(b_last - b)).astype(BF16)
            dc_ref[pl.ds(ci, 1), :] = jnp.exp(b_last)
        return carry

    lax.fori_loop(0, tm // c, chunk_body, 0)


def _inproj(x2, nw, w_in, cos, sin, lbl, seq_len):
    n = x2.shape[0]
    tm = TOKEN_TILE
    tiles_per_seq = seq_len // tm
    cpt = tm // HGRN_CHUNK
    row = lambda i: (i, 0)
    pos = lambda i: (i % tiles_per_seq, 0)
    bf = lambda w: jax.ShapeDtypeStruct((n, w), BF16)
    dec = jax.ShapeDtypeStruct((n // HGRN_CHUNK, HGRN_WIDTH), F32)
    tok = lambda w: pl.BlockSpec((tm, w), row)
    dspec = pl.BlockSpec((cpt, HGRN_WIDTH), row)
    hg4 = [bf(HGRN_WIDTH)] * 4
    hs4 = [tok(HGRN_WIDTH)] * 4
    return pl.pallas_call(
        _inproj_kernel,
        grid=(n // tm,),
        in_specs=[tok(D_MODEL), _const_spec((1, D_MODEL)), _const_spec((D_MODEL, D_IN)),
                  pl.BlockSpec((tm, LANES), pos), pl.BlockSpec((tm, LANES), pos),
                  _const_spec(lbl.shape)],
        out_specs=[tok(ATTN_WIDTH), tok(KV_WIDTH), tok(KV_WIDTH), tok(HGRN_WIDTH), tok(HGRN_WIDTH),
                   tok(D_MODEL), tok(D_MODEL)] + hs4 + [dspec] + hs4 + [dspec],
        out_shape=[bf(ATTN_WIDTH), bf(KV_WIDTH), bf(KV_WIDTH), bf(HGRN_WIDTH), bf(HGRN_WIDTH),
                   bf(D_MODEL), bf(D_MODEL)] + hg4 + [dec] + hg4 + [dec],
        scratch_shapes=[pltpu.VMEM((tm, 3 * HGRN_WIDTH), F32)],
        compiler_params=pltpu.CompilerParams(
            dimension_semantics=("arbitrary",), vmem_limit_bytes=VMEM_LIMIT_BYTES),
        name="inproj",
    )(x2, nw, w_in, cos, sin, lbl)


def _attn_kernel(sink_ref, q_ref, kp_ref, kc_ref, kn_ref, vp_ref, vc_ref, vn_ref, o_ref,
                 kw_ref, vw_ref):
    i = pl.program_id(1)
    nblk = pl.num_programs(1)
    tq = q_ref.shape[0]
    w = WINDOW
    kw_ref[0:w, :] = kp_ref[...]
    kw_ref[w:w + tq, :] = kc_ref[...]
    kw_ref[w + tq:, :] = kn_ref[...]
    vw_ref[0:w, :] = vp_ref[...]
    vw_ref[w:w + tq, :] = vc_ref[...]
    vw_ref[w + tq:, :] = vn_ref[...]

    lane = lax.broadcasted_iota(jnp.int32, (w, LANES), 1)
    lo_half = lane < HEAD_DIM
    lane_v = lax.broadcasted_iota(jnp.int32, (3 * w, LANES), 1)
    lo_half_v = lane_v < HEAD_DIM
    r = lax.broadcasted_iota(jnp.int32, (w, 3 * w), 0)
    cc = lax.broadcasted_iota(jnp.int32, (w, 3 * w), 1)
    band = jnp.abs(r + w - cc) <= w
    n_pairs = N_Q_HEADS // 2

    for s in range(tq // w):
        valid = band
        if s == 0:
            valid = jnp.logical_and(valid, jnp.logical_or(cc >= w, i != 0))
        if s == tq // w - 1:
            valid = jnp.logical_and(valid, jnp.logical_or(cc < 2 * w, i != nblk - 1))
        kwin = kw_ref[s * w:s * w + 3 * w, :]
        vwin = vw_ref[s * w:s * w + 3 * w, :]
        zero_v = jnp.zeros_like(vwin)
        vblk = jnp.concatenate([jnp.where(lo_half_v, vwin, zero_v),
                                jnp.where(lo_half_v, zero_v, vwin)], axis=0)
        parts = []
        for j in range(n_pairs):
            qj = q_ref[s * w:(s + 1) * w, j * LANES:(j + 1) * LANES]
            zero_q = jnp.zeros_like(qj)
            parts.append(jnp.where(lo_half, qj, zero_q))
            parts.append(jnp.where(lo_half, zero_q, qj))
        lhs = jnp.concatenate(parts, axis=0)
        sc = lax.dot_general(lhs, kwin, (((1,), (1,)), ((), ())),
                             preferred_element_type=F32)
        for j in range(n_pairs):
            pn = []
            for t in range(2):
                head = j + t * n_pairs
                blk = 2 * j + t
                sb = jnp.where(valid, sc[blk * w:(blk + 1) * w, :], -jnp.inf)
                sink = sink_ref[head]
                m = jnp.maximum(jnp.max(sb, axis=-1, keepdims=True), sink)
                p = jnp.exp(sb - m)
                denom = jnp.sum(p, axis=-1, keepdims=True) + jnp.exp(sink - m)
                pn.append((p * (1.0 / denom)).astype(BF16))
            pcat = jnp.concatenate(pn, axis=1)
            o = jnp.dot(pcat, vblk, preferred_element_type=F32)
            o_ref[s * w:(s + 1) * w, j * LANES:(j + 1) * LANES] = o.astype(BF16)


def _attn(sink, q, k, v, batch, seq_len):
    tq = ATTN_TILE
    w = WINDOW
    nblk = seq_len // tq
    sub = tq // w
    nrow_blocks = batch * seq_len // w
    cur = lambda b, i, *_: (b * nblk + i, 0)
    prev = lambda b, i, *_: (jnp.maximum((b * nblk + i) * sub - 1, 0), 0)
    nxt = lambda b, i, *_: (jnp.minimum((b * nblk + i + 1) * sub, nrow_blocks - 1), 0)
    kv_cur = pl.BlockSpec((tq, KV_WIDTH), cur)
    kv_prev = pl.BlockSpec((w, KV_WIDTH), prev)
    kv_next = pl.BlockSpec((w, KV_WIDTH), nxt)
    return pl.pallas_call(
        _attn_kernel,
        grid_spec=pltpu.PrefetchScalarGridSpec(
            num_scalar_prefetch=1,
            grid=(batch, nblk),
            in_specs=[pl.BlockSpec((tq, ATTN_WIDTH), cur),
                      kv_prev, kv_cur, kv_next, kv_prev, kv_cur, kv_next],
            out_specs=pl.BlockSpec((tq, ATTN_WIDTH), cur),
            scratch_shapes=[pltpu.VMEM((tq + 2 * w, KV_WIDTH), BF16),
                            pltpu.VMEM((tq + 2 * w, KV_WIDTH), BF16)],
        ),
        out_shape=jax.ShapeDtypeStruct(q.shape, BF16),
        compiler_params=pltpu.CompilerParams(
            dimension_semantics=("arbitrary", "arbitrary"), vmem_limit_bytes=VMEM_LIMIT_BYTES),
        name="attn",
    )(sink, q, k, k, k, v, v, v)


def _hgrn_kernel(qd_f_ref, kd_f_ref, qx_f_ref, kx_f_ref, dc_f_ref, v_f_ref,
                 qd_b_ref, kd_b_ref, qx_b_ref, kx_b_ref, dc_b_ref, v_b_ref,
                 of_ref, ob_ref, s_ref):
    @pl.when(pl.program_id(1) == 0)
    def _():
        s_ref[...] = jnp.zeros_like(s_ref)

    c = HGRN_CHUNK
    n_chunks = of_ref.shape[0] // c
    r = lax.broadcasted_iota(jnp.int32, (c, c), 0)
    cc = lax.broadcasted_iota(jnp.int32, (c, c), 1)
    keep = (r >= cc, r <= cc)
    dirs = ((qd_f_ref, kd_f_ref, qx_f_ref, kx_f_ref, dc_f_ref, v_f_ref, of_ref),
            (qd_b_ref, kd_b_ref, qx_b_ref, kx_b_ref, dc_b_ref, v_b_ref, ob_ref))
    nt = (((1,), (1,)), ((), ()))
    tn = (((0,), (0,)), ((), ()))

    def step(t, carry):
        for d, (qd_ref, kd_ref, qx_ref, kx_ref, dc_ref, v_ref, o_ref) in enumerate(dirs):
            ci = t if d == 0 else n_chunks - 1 - t
            rows = pl.ds(pl.multiple_of(ci * c, c), c)
            dc = dc_ref[pl.ds(ci, 1), :]
            for hd in range(HGRN_HEADS):
                cols = slice(hd * HGRN_DK, (hd + 1) * HGRN_DK)
                idx = d * HGRN_HEADS + hd
                vh = v_ref[rows, cols]
                a = lax.dot_general(qd_ref[rows, cols], kd_ref[rows, cols], nt,
                                    preferred_element_type=F32)
                a = jnp.where(keep[d], a, 0.0).astype(BF16)
                st = s_ref[idx]
                o = jnp.dot(a, vh, preferred_element_type=F32)
                o = o + lax.dot_general(qx_ref[rows, cols], st.astype(BF16), nt,
                                        preferred_element_type=F32)
                o_ref[rows, cols] = o
                s_ref[idx] = st * dc[:, cols] + lax.dot_general(
                    vh, kx_ref[rows, cols], tn, preferred_element_type=F32)
        return carry

    lax.fori_loop(0, n_chunks, step, 0)


def _hgrn(prep_f, prep_b, v, batch, seq_len):
    tm = TOKEN_TILE
    nblk = seq_len // tm
    cpt = tm // HGRN_CHUNK
    fwd = lambda b, i: (b * nblk + i, 0)
    bwd = lambda b, i: (b * nblk + nblk - 1 - i, 0)
    n = v.shape[0]

    def specs(imap):
        t = pl.BlockSpec((tm, HGRN_WIDTH), imap)
        return [t, t, t, t, pl.BlockSpec((cpt, HGRN_WIDTH), imap), t]

    out = jax.ShapeDtypeStruct((n, HGRN_WIDTH), F32)
    return pl.pallas_call(
        _hgrn_kernel,
        grid=(batch, nblk),
        in_specs=specs(fwd) + specs(bwd),
        out_specs=[pl.BlockSpec((tm, HGRN_WIDTH), fwd), pl.BlockSpec((tm, HGRN_WIDTH), bwd)],
        out_shape=[out, out],
        scratch_shapes=[pltpu.VMEM((2 * HGRN_HEADS, HGRN_DK, HGRN_DK), F32)],
        compiler_params=pltpu.CompilerParams(
            dimension_semantics=("arbitrary", "arbitrary"), vmem_limit_bytes=VMEM_LIMIT_BYTES),
        name="hgrn",
    )(*prep_f, v, *prep_b, v)


def _mix_kernel(x_ref, a_ref, of_ref, ob_ref, hg_ref, ga_ref, gh_ref,
                gw_ref, wa_ref, wh_ref, wm_ref, x1_ref):
    o = of_ref[...] + ob_ref[...]
    hg = hg_ref[...].astype(F32)
    gw = gw_ref[...]
    parts = []
    for hd in range(HGRN_HEADS):
        cols = slice(hd * HGRN_DK, (hd + 1) * HGRN_DK)
        oh = o[:, cols]
        ms = jnp.mean(oh * oh, axis=-1, keepdims=True)
        parts.append(oh * lax.rsqrt(ms + NORM_EPS))
    on = jnp.concatenate(parts, axis=1)
    rr = (on * gw * (hg * jax.nn.sigmoid(hg))).astype(BF16)
    pa = jnp.dot(a_ref[...], wa_ref[...], preferred_element_type=F32)
    ph = jnp.dot(rr, wh_ref[...], preferred_element_type=F32)
    mixed = (jax.nn.sigmoid(ga_ref[...].astype(F32)) * pa
             + jax.nn.sigmoid(gh_ref[...].astype(F32)) * ph).astype(BF16)
    x1_ref[...] = x_ref[...] + jnp.dot(mixed, wm_ref[...], preferred_element_type=F32)


def _mix(x2, a, of, ob, hg, ga, gh, gw, wa, wh, wm):
    n = x2.shape[0]
    tm = TOKEN_TILE
    row = lambda i: (i, 0)
    tok = lambda w: pl.BlockSpec((tm, w), row)
    return pl.pallas_call(
        _mix_kernel,
        grid=(n // tm,),
        in_specs=[tok(D_MODEL), tok(ATTN_WIDTH), tok(HGRN_WIDTH), tok(HGRN_WIDTH), tok(HGRN_WIDTH),
                  tok(D_MODEL), tok(D_MODEL),
                  _const_spec((1, HGRN_WIDTH)), _const_spec((ATTN_WIDTH, D_MODEL)),
                  _const_spec((HGRN_WIDTH, D_MODEL)), _const_spec((D_MODEL, D_MODEL))],
        out_specs=tok(D_MODEL),
        out_shape=jax.ShapeDtypeStruct((n, D_MODEL), F32),
        compiler_params=pltpu.CompilerParams(
            dimension_semantics=("arbitrary",), vmem_limit_bytes=VMEM_LIMIT_BYTES),
        name="mix",
    )(x2, a, of, ob, hg, ga, gh, gw, wa, wh, wm)


def _ffn_kernel(xp_ref, xc_ref, xn_ref, nw_ref, wu_ref, cw_ref, cb_ref, wd_ref, fw_ref,
                y_ref, hb_ref, hbf_ref, u_ref, act_ref, *, tiles_per_seq):
    i = pl.program_id(0)
    tm = xc_ref.shape[0]
    hl = SUBLANES
    nw = nw_ref[...]
    pos = i % tiles_per_seq
    keep_prev = jnp.where(pos != 0, 1.0, 0.0)
    keep_next = jnp.where(pos != tiles_per_seq - 1, 1.0, 0.0)
    hb_ref[0:hl, :] = _rms(xp_ref[...], nw) * keep_prev
    hb_ref[hl:hl + tm, :] = _rms(xc_ref[...], nw)
    hb_ref[hl + tm:, :] = _rms(xn_ref[...], nw) * keep_next
    hbf_ref[...] = hb_ref[...].astype(BF16)

    ck = FFN_COL_CHUNK
    nck = D_FF // ck
    nlb = ck // LANES

    def up(j):
        lo, hi = j * ck, (j + 1) * ck
        uv = jnp.dot(hbf_ref[...], wu_ref[:, lo:hi], preferred_element_type=F32)
        ug = jnp.dot(hbf_ref[...], wu_ref[:, D_FF + lo:D_FF + hi], preferred_element_type=F32)
        for k in range(nlb):
            u_ref[j, k] = uv[:, k * LANES:(k + 1) * LANES]
            u_ref[j, nlb + k] = ug[:, k * LANES:(k + 1) * LANES]

    def conv(j, k, col):
        cs = slice(col, col + LANES)
        return (cw_ref[0:1, cs] * u_ref[j, k, hl - 1:hl - 1 + tm, :]
                + cw_ref[1:2, cs] * u_ref[j, k, hl:hl + tm, :]
                + cw_ref[2:3, cs] * u_ref[j, k, hl + 1:hl + 1 + tm, :]
                + cb_ref[:, cs])

    def act(j):
        for k in range(nlb):
            col = j * ck + k * LANES
            val = conv(j, k, col)
            gate = conv(j, nlb + k, D_FF + col)
            act_ref[:, col:col + LANES] = (gate * jax.nn.sigmoid(gate) * val).astype(BF16)

    up(0)
    for j in range(nck):
        if j + 1 < nck:
            up(j + 1)
        act(j)
    y = xc_ref[...] + jnp.dot(act_ref[...], wd_ref[...], preferred_element_type=F32)
    y_ref[...] = _rms(y, fw_ref[...])


def _ffn(x1, nw, wu, cw, cb, wd, fw, seq_len):
    n = x1.shape[0]
    tm = TOKEN_TILE
    hl = SUBLANES
    per = tm // hl
    nhalo = n // hl
    row = lambda i: (i, 0)
    prev = lambda i: (jnp.maximum(i * per - 1, 0), 0)
    nxt = lambda i: (jnp.minimum((i + 1) * per, nhalo - 1), 0)
    kern = functools.partial(_ffn_kernel, tiles_per_seq=seq_len // tm)
    return pl.pallas_call(
        kern,
        grid=(n // tm,),
        in_specs=[pl.BlockSpec((hl, D_MODEL), prev), pl.BlockSpec((tm, D_MODEL), row),
                  pl.BlockSpec((hl, D_MODEL), nxt),
                  _const_spec((1, D_MODEL)), _const_spec((D_MODEL, 2 * D_FF)),
                  _const_spec((3, 2 * D_FF)), _const_spec((1, 2 * D_FF)),
                  _const_spec((D_FF, D_MODEL)), _const_spec((1, D_MODEL))],
        out_specs=pl.BlockSpec((tm, D_MODEL), row),
        out_shape=jax.ShapeDtypeStruct((n, D_MODEL), F32),
        scratch_shapes=[pltpu.VMEM((tm + 2 * hl, D_MODEL), F32),
                        pltpu.VMEM((tm + 2 * hl, D_MODEL), BF16),
                        pltpu.VMEM((D_FF // FFN_COL_CHUNK, 2 * FFN_COL_CHUNK // LANES, tm + 2 * hl, LANES), F32),
                        pltpu.VMEM((tm, D_FF), BF16)],
        compiler_params=pltpu.CompilerParams(
            dimension_semantics=("arbitrary",), vmem_limit_bytes=VMEM_LIMIT_BYTES),
        name="ffn",
    )(x1, x1, x1, nw, wu, cw, cb, wd, fw)


def _head_pair_perm():
    half = N_Q_HEADS // 2
    order = []
    for j in range(half):
        order += [j, j + half]
    return np.concatenate([np.arange(h * HEAD_DIM, (h + 1) * HEAD_DIM) for h in order])


def _rope_tables(seq_len):
    half = HEAD_DIM // 2
    inv_freq = ROPE_THETA ** (-jnp.arange(half, dtype=F32) / half)
    ang = jnp.arange(seq_len, dtype=F32)[:, None] * inv_freq[None, :]
    cos = jnp.cos(ang)
    sin = jnp.sin(ang)
    reps = LANES // HEAD_DIM
    cos_t = jnp.tile(jnp.concatenate([cos, cos], axis=1), (1, reps))
    sin_t = jnp.tile(jnp.concatenate([-sin, sin], axis=1), (1, reps))
    return cos_t, sin_t


def _trunk(x, p):
    batch, seq_len, _ = x.shape
    n = batch * seq_len
    x2 = x.reshape(n, D_MODEL)
    cos, sin = _rope_tables(seq_len)
    (q, k, v, hi, hg, ga, gh,
     qd_f, kd_f, qx_f, kx_f, dc_f,
     qd_b, kd_b, qx_b, kx_b, dc_b) = _inproj(x2, p["norm_mix_w"], p["w_in"], cos, sin, p["lbl"], seq_len)
    a = _attn(p["sink"], q, k, v, batch, seq_len)
    of, ob = _hgrn((qd_f, kd_f, qx_f, kx_f, dc_f), (qd_b, kd_b, qx_b, kx_b, dc_b), hi, batch, seq_len)
    x1 = _mix(x2, a, of, ob, hg, ga, gh, p["hgrn_norm_w"], p["w_attn_out"], p["w_hgrn_out"], p["w_mix_out"])
    y = _ffn(x1, p["norm_ffn_w"], p["w_up"], p["conv_w"], p["conv_b"], p["w_down"], p["norm_final_w"], seq_len)
    return y.reshape(batch, seq_len, D_MODEL)


def kernel(x_prompt, x_sample, norm_mix_w, w_in, attn_sink, hgrn_lb_logits, hgrn_norm_w, w_attn_out, w_hgrn_out, w_mix_out, norm_ffn_w, w_up, conv_w, conv_b, w_down, norm_final_w):
    perm = _head_pair_perm()
    w_in0 = w_in[0]
    w_in_p = jnp.concatenate([w_in0[:, :ATTN_WIDTH][:, perm], w_in0[:, ATTN_WIDTH:]], axis=1)
    p = {
        "norm_mix_w": norm_mix_w[0].reshape(1, D_MODEL),
        "w_in": w_in_p.astype(BF16),
        "sink": attn_sink[0].astype(F32),
        "lbl": hgrn_lb_logits.astype(F32).reshape(hgrn_lb_logits.shape[0], 2 * HGRN_WIDTH),
        "hgrn_norm_w": hgrn_norm_w[0].astype(F32).reshape(1, HGRN_WIDTH),
        "w_attn_out": w_attn_out[0][perm, :].astype(BF16),
        "w_hgrn_out": w_hgrn_out[0].astype(BF16),
        "w_mix_out": w_mix_out[0].astype(BF16),
        "norm_ffn_w": norm_ffn_w[0].reshape(1, D_MODEL),
        "w_up": w_up[0].astype(BF16),
        "conv_w": conv_w[0],
        "conv_b": conv_b[0].reshape(1, 2 * D_FF),
        "w_down": w_down[0].astype(BF16),
        "norm_final_w": norm_final_w.reshape(1, D_MODEL),
    }
    return (_trunk(x_prompt, p), _trunk(x_sample, p))
```

```python
import functools

import jax
import jax.numpy as jnp
import numpy as np
from jax import lax
from jax.experimental import pallas as pl
from jax.experimental.pallas import tpu as pltpu

D_MODEL = 1024
HEAD_DIM = 64
N_Q_HEADS = 8
N_KV_HEADS = 2
ATTN_WIDTH = N_Q_HEADS * HEAD_DIM
KV_WIDTH = N_KV_HEADS * HEAD_DIM
WINDOW = 128
ROPE_THETA = 10000.0
HGRN_HEADS = 4
HGRN_DK = 128
HGRN_WIDTH = HGRN_HEADS * HGRN_DK
HGRN_CHUNK = 64
D_FF = 2816
NORM_EPS = 1e-6

_OFF_Q = 0
_OFF_K = _OFF_Q + ATTN_WIDTH
_OFF_V = _OFF_K + KV_WIDTH
_OFF_HQ = _OFF_V + KV_WIDTH
_OFF_HFF = _OFF_HQ + HGRN_WIDTH
_OFF_HFB = _OFF_HFF + HGRN_WIDTH
_OFF_HI = _OFF_HFB + HGRN_WIDTH
_OFF_HG = _OFF_HI + HGRN_WIDTH
_OFF_GA = _OFF_HG + HGRN_WIDTH
_OFF_GH = _OFF_GA + D_MODEL
D_IN = _OFF_GH + D_MODEL

LANES = 128
SUBLANES = 8
VMEM_LIMIT_BYTES = 56 * 1024 * 1024

TOKEN_TILE = 512
ATTN_TILE = 512
FFN_COL_CHUNK = 256

BF16 = jnp.bfloat16
F32 = jnp.float32


def _const_spec(shape):
    nd = len(shape)
    return pl.BlockSpec(shape, lambda *_: (0,) * nd, pipeline_mode=pl.Buffered(1))


def _rms(x, w):
    ms = jnp.mean(x * x, axis=-1, keepdims=True)
    return x * lax.rsqrt(ms + NORM_EPS) * w


def _rotate_half_pairs(x):
    lane = lax.broadcasted_iota(jnp.int32, x.shape, 1)
    first_half = (lane % HEAD_DIM) < (HEAD_DIM // 2)
    fwd = pltpu.roll(x, LANES - HEAD_DIM // 2, 1)
    bwd = pltpu.roll(x, HEAD_DIM // 2, 1)
    return jnp.where(first_half, fwd, bwd)


def _inproj_kernel(x_ref, nw_ref, w_ref, cos_ref, sin_ref,
                   q_ref, k_ref, v_ref, zh_ref, hi_ref, hg_ref, ga_ref, gh_ref, h_ref):
    h_ref[...] = _rms(x_ref[...], nw_ref[...]).astype(BF16)

    def proj(lo, hi):
        return jnp.dot(h_ref[...], w_ref[:, lo:hi], preferred_element_type=F32)

    def rope(z):
        return z * cos_ref[...] + _rotate_half_pairs(z) * sin_ref[...]

    zq = proj(_OFF_Q, _OFF_K)
    for j in range(ATTN_WIDTH // LANES):
        sl = slice(j * LANES, (j + 1) * LANES)
        q_ref[:, sl] = (rope(zq[:, sl]) * (HEAD_DIM ** -0.5)).astype(BF16)
    k_ref[...] = rope(proj(_OFF_K, _OFF_V)).astype(BF16)
    v_ref[...] = proj(_OFF_V, _OFF_HQ).astype(BF16)
    zh_ref[...] = proj(_OFF_HQ, _OFF_HI)
    hi_ref[...] = proj(_OFF_HI, _OFF_HG).astype(BF16)
    hg_ref[...] = proj(_OFF_HG, _OFF_GA).astype(BF16)
    half = D_MODEL // 2
    for o_ref, off in ((ga_ref, _OFF_GA), (gh_ref, _OFF_GH)):
        for u in range(2):
            o_ref[:, u * half:(u + 1) * half] = proj(off + u * half, off + (u + 1) * half).astype(BF16)


def _inproj(x2, nw, w_in, cos, sin, seq_len):
    n = x2.shape[0]
    tm = TOKEN_TILE
    tiles_per_seq = seq_len // tm
    row = lambda i: (i, 0)
    pos = lambda i: (i % tiles_per_seq, 0)
    bf = lambda w: jax.ShapeDtypeStruct((n, w), BF16)
    tok = lambda w: pl.BlockSpec((tm, w), row)
    widths = (ATTN_WIDTH, KV_WIDTH, KV_WIDTH, 3 * HGRN_WIDTH, HGRN_WIDTH, HGRN_WIDTH, D_MODEL, D_MODEL)
    shapes = [bf(w) for w in widths]
    shapes[3] = jax.ShapeDtypeStruct((n, 3 * HGRN_WIDTH), F32)
    return pl.pallas_call(
        _inproj_kernel,
        grid=(n // tm,),
        in_specs=[tok(D_MODEL), _const_spec((1, D_MODEL)), _const_spec((D_MODEL, D_IN)),
                  pl.BlockSpec((tm, LANES), pos), pl.BlockSpec((tm, LANES), pos)],
        out_specs=[tok(w) for w in widths],
        out_shape=shapes,
        scratch_shapes=[pltpu.VMEM((tm, D_MODEL), BF16)],
        compiler_params=pltpu.CompilerParams(
            dimension_semantics=("arbitrary",), vmem_limit_bytes=VMEM_LIMIT_BYTES),
        name="inproj",
    )(x2, nw, w_in, cos, sin)


def _attn_kernel(sink_ref, q_ref, kp_ref, kc_ref, kn_ref, vp_ref, vc_ref, vn_ref, o_ref,
                 kw_ref, vw_ref):
    i = pl.program_id(1)
    nblk = pl.num_programs(1)
    tq = q_ref.shape[0]
    w = WINDOW
    kw_ref[0:w, :] = kp_ref[...]
    kw_ref[w:w + tq, :] = kc_ref[...]
    kw_ref[w + tq:, :] = kn_ref[...]
    vw_ref[0:w, :] = vp_ref[...]
    vw_ref[w:w + tq, :] = vc_ref[...]
    vw_ref[w + tq:, :] = vn_ref[...]

    lane = lax.broadcasted_iota(jnp.int32, (w, LANES), 1)
    lo_half = lane < HEAD_DIM
    lane_v = lax.broadcasted_iota(jnp.int32, (3 * w, LANES), 1)
    lo_half_v = lane_v < HEAD_DIM
    r = lax.broadcasted_iota(jnp.int32, (w, 3 * w), 0)
    cc = lax.broadcasted_iota(jnp.int32, (w, 3 * w), 1)
    band = jnp.abs(r + w - cc) <= w
    n_pairs = N_Q_HEADS // 2

    for s in range(tq // w):
        valid = band
        if s == 0:
            valid = jnp.logical_and(valid, jnp.logical_or(cc >= w, i != 0))
        if s == tq // w - 1:
            valid = jnp.logical_and(valid, jnp.logical_or(cc < 2 * w, i != nblk - 1))
        kwin = kw_ref[s * w:s * w + 3 * w, :]
        vwin = vw_ref[s * w:s * w + 3 * w, :]
        zero_v = jnp.zeros_like(vwin)
        vblk = jnp.concatenate([jnp.where(lo_half_v, vwin, zero_v),
                                jnp.where(lo_half_v, zero_v, vwin)], axis=0)
        parts = []
        for j in range(n_pairs):
            qj = q_ref[s * w:(s + 1) * w, j * LANES:(j + 1) * LANES]
            zero_q = jnp.zeros_like(qj)
            parts.append(jnp.where(lo_half, qj, zero_q))
            parts.append(jnp.where(lo_half, zero_q, qj))
        lhs = jnp.concatenate(parts, axis=0)
        sc = lax.dot_general(lhs, kwin, (((1,), (1,)), ((), ())),
                             preferred_element_type=F32)
        for j in range(n_pairs):
            pn = []
            for t in range(2):
                head = j + t * n_pairs
                blk = 2 * j + t
                sb = jnp.where(valid, sc[blk * w:(blk + 1) * w, :], -jnp.inf)
                sink = sink_ref[head]
                m = jnp.maximum(jnp.max(sb, axis=-1, keepdims=True), sink)
                p = jnp.exp(sb - m)
                denom = jnp.sum(p, axis=-1, keepdims=True) + jnp.exp(sink - m)
                pn.append((p * (1.0 / denom)).astype(BF16))
            pcat = jnp.concatenate(pn, axis=1)
            o = jnp.dot(pcat, vblk, preferred_element_type=F32)
            o_ref[s * w:(s + 1) * w, j * LANES:(j + 1) * LANES] = o.astype(BF16)


def _attn(sink, q, k, v, batch, seq_len):
    tq = ATTN_TILE
    w = WINDOW
    nblk = seq_len // tq
    sub = tq // w
    nrow_blocks = batch * seq_len // w
    cur = lambda b, i, *_: (b * nblk + i, 0)
    prev = lambda b, i, *_: (jnp.maximum((b * nblk + i) * sub - 1, 0), 0)
    nxt = lambda b, i, *_: (jnp.minimum((b * nblk + i + 1) * sub, nrow_blocks - 1), 0)
    kv_cur = pl.BlockSpec((tq, KV_WIDTH), cur)
    kv_prev = pl.BlockSpec((w, KV_WIDTH), prev)
    kv_next = pl.BlockSpec((w, KV_WIDTH), nxt)
    return pl.pallas_call(
        _attn_kernel,
        grid_spec=pltpu.PrefetchScalarGridSpec(
            num_scalar_prefetch=1,
            grid=(batch, nblk),
            in_specs=[pl.BlockSpec((tq, ATTN_WIDTH), cur),
                      kv_prev, kv_cur, kv_next, kv_prev, kv_cur, kv_next],
            out_specs=pl.BlockSpec((tq, ATTN_WIDTH), cur),
            scratch_shapes=[pltpu.VMEM((tq + 2 * w, KV_WIDTH), BF16),
                            pltpu.VMEM((tq + 2 * w, KV_WIDTH), BF16)],
        ),
        out_shape=jax.ShapeDtypeStruct(q.shape, BF16),
        compiler_params=pltpu.CompilerParams(
            dimension_semantics=("arbitrary", "arbitrary"), vmem_limit_bytes=VMEM_LIMIT_BYTES),
        name="attn",
    )(sink, q, k, k, k, v, v, v)


def _hgrn_kernel(hq_f_ref, xf_f_ref, v_f_ref, hq_b_ref, xf_b_ref, v_b_ref, lbl_ref,
                 of_ref, ob_ref,
                 s_ref, sbf_ref, p_ref, dc_ref, a_ref, u_ref, gp_ref, kk_ref, b_ref):
    @pl.when(pl.program_id(1) == 0)
    def _():
        s_ref[...] = jnp.zeros_like(s_ref)
        sbf_ref[...] = jnp.zeros_like(sbf_ref)

    c = HGRN_CHUNK
    n_chunks = of_ref.shape[0] // c
    r = lax.broadcasted_iota(jnp.int32, (c, c), 0)
    cc = lax.broadcasted_iota(jnp.int32, (c, c), 1)
    keep = (r >= cc, r <= cc)
    tri = (keep[0].astype(BF16), keep[1].astype(BF16))
    nt = (((1,), (1,)), ((), ()))
    tn = (((0,), (0,)), ((), ()))
    ins = ((hq_f_ref, xf_f_ref, v_f_ref, of_ref), (hq_b_ref, xf_b_ref, v_b_ref, ob_ref))
    n_strips = HGRN_WIDTH // LANES

    lbl = lbl_ref[...]
    e = jnp.exp(lbl - jnp.max(lbl, axis=0, keepdims=True))
    lb = e[0:1, :] / jnp.sum(e, axis=0, keepdims=True)

    def chunk_rows(d, t):
        ci = t if d == 0 else n_chunks - 1 - t
        return slice(ci * c, (ci + 1) * c)

    def stage1(t, d, blk):
        rows = chunk_rows(d, t)
        cols = slice(blk * LANES, (blk + 1) * LANES)
        wide = slice(d * HGRN_WIDTH + blk * LANES, d * HGRN_WIDTH + (blk + 1) * LANES)
        xf = ins[d][1][rows, cols]
        lbd = lb[:, wide]
        f = lbd + (1.0 - lbd) * jax.nn.sigmoid(xf)
        kk_ref[:, wide] = 1.0 - f
        g = jnp.log(f)
        g1 = g.astype(BF16)
        r1 = g - g1.astype(F32)
        g2 = r1.astype(BF16)
        gp_ref[0, :, wide] = g1
        gp_ref[1, :, wide] = g2
        gp_ref[2, :, wide] = (r1 - g2.astype(F32)).astype(BF16)

    def cumsum(d):
        wide = slice(d * HGRN_WIDTH, (d + 1) * HGRN_WIDTH)
        b_ref[:, wide] = (jnp.dot(tri[d], gp_ref[0, :, wide], preferred_element_type=F32)
                          + jnp.dot(tri[d], gp_ref[1, :, wide], preferred_element_type=F32)
                          + jnp.dot(tri[d], gp_ref[2, :, wide], preferred_element_type=F32))

    def stage3(slot, t, d, blk):
        rows = chunk_rows(d, t)
        cols = slice(blk * LANES, (blk + 1) * LANES)
        wide = slice(d * HGRN_WIDTH + blk * LANES, d * HGRN_WIDTH + (blk + 1) * LANES)
        hq = ins[d][0][rows, cols]
        b = b_ref[:, wide]
        if d == 0:
            ref = b[c // 2 - 1:c // 2, :]
            b_last = b[c - 1:c, :]
        else:
            ref = b[c // 2:c // 2 + 1, :]
            b_last = b[0:1, :]
        qd = hq * jnp.exp(b - ref)
        kd = kk_ref[:, wide] * jnp.exp(ref - b)
        p_ref[slot, d, 0, :, cols] = qd.astype(BF16)
        p_ref[slot, d, 1, :, cols] = kd.astype(BF16)
        p_ref[slot, d, 2, :, cols] = (qd * jnp.exp(ref)).astype(BF16)
        p_ref[slot, d, 3, :, cols] = (kd * jnp.exp(b_last - ref)).astype(BF16)
        dc_ref[slot, d, 0:1, cols] = jnp.exp(b_last)

    units = [(d, hd) for d in range(2) for hd in range(HGRN_HEADS)]
    strips = [(d, blk) for blk in range(n_strips) for d in range(2)]

    def step(slot, t, prep_next):
        if prep_next:
            for d, blk in strips:
                stage1(t + 1, d, blk)
        for d, hd in units:
            cols = slice(hd * HGRN_DK, (hd + 1) * HGRN_DK)
            a = lax.dot_general(p_ref[slot, d, 0, :, cols], p_ref[slot, d, 1, :, cols], nt,
                                preferred_element_type=F32)
            a_ref[d * HGRN_HEADS + hd] = jnp.where(keep[d], a, 0.0).astype(BF16)
        if prep_next:
            cumsum(0)
            cumsum(1)
        for d, hd in units:
            cols = slice(hd * HGRN_DK, (hd + 1) * HGRN_DK)
            vh = ins[d][2][chunk_rows(d, t), cols]
            u_ref[d * HGRN_HEADS + hd] = lax.dot_general(vh, p_ref[slot, d, 3, :, cols], tn,
                                                         preferred_element_type=F32)
        for d, hd in units:
            cols = slice(hd * HGRN_DK, (hd + 1) * HGRN_DK)
            idx = d * HGRN_HEADS + hd
            rows = chunk_rows(d, t)
            o = lax.dot_general(p_ref[slot, d, 2, :, cols], sbf_ref[idx], nt, preferred_element_type=F32)
            o = o + jnp.dot(a_ref[idx], ins[d][2][rows, cols], preferred_element_type=F32)
            ins[d][3][rows, cols] = o.astype(ins[d][3].dtype)
        if prep_next:
            for d, blk in strips:
                stage3(1 - slot, t + 1, d, blk)
        for d, hd in units:
            cols = slice(hd * HGRN_DK, (hd + 1) * HGRN_DK)
            idx = d * HGRN_HEADS + hd
            st = s_ref[idx] * dc_ref[slot, d, 0:1, cols] + u_ref[idx]
            s_ref[idx] = st
            sbf_ref[idx] = st.astype(BF16)

    for d, blk in strips:
        stage1(0, d, blk)
    cumsum(0)
    cumsum(1)
    for d, blk in strips:
        stage3(0, 0, d, blk)
    for t in range(n_chunks):
        step(t % 2, t, t + 1 < n_chunks)


def _hgrn(zh, v, lbl, batch, seq_len):
    tm = TOKEN_TILE
    nblk = seq_len // tm
    n = v.shape[0]
    w = HGRN_WIDTH
    fwd = lambda col: (lambda b, i: (b * nblk + i, col))
    bwd = lambda col: (lambda b, i: (b * nblk + nblk - 1 - i, col))
    blk = lambda imap: pl.BlockSpec((tm, w), imap)
    out = jax.ShapeDtypeStruct((n, w), BF16)
    nu = 2 * HGRN_HEADS
    return pl.pallas_call(
        _hgrn_kernel,
        grid=(batch, nblk),
        in_specs=[blk(fwd(0)), blk(fwd(1)), blk(fwd(0)), blk(bwd(0)), blk(bwd(2)), blk(bwd(0)),
                  _const_spec(lbl.shape)],
        out_specs=[blk(fwd(0)), blk(bwd(0))],
        out_shape=[out, out],
        scratch_shapes=[pltpu.VMEM((nu, HGRN_DK, HGRN_DK), F32),
                        pltpu.VMEM((nu, HGRN_DK, HGRN_DK), BF16),
                        pltpu.VMEM((2, 2, 4, HGRN_CHUNK, w), BF16),
                        pltpu.VMEM((2, 2, SUBLANES, w), F32),
                        pltpu.VMEM((nu, HGRN_CHUNK, HGRN_CHUNK), BF16),
                        pltpu.VMEM((nu, HGRN_DK, HGRN_DK), F32),
                        pltpu.VMEM((3, HGRN_CHUNK, 2 * w), BF16),
                        pltpu.VMEM((HGRN_CHUNK, 2 * w), F32),
                        pltpu.VMEM((HGRN_CHUNK, 2 * w), F32)],
        compiler_params=pltpu.CompilerParams(
            dimension_semantics=("arbitrary", "arbitrary"), vmem_limit_bytes=VMEM_LIMIT_BYTES),
        name="hgrn",
    )(zh, zh, v, zh, zh, v, lbl)


def _mix_kernel(x_ref, a_ref, of_ref, ob_ref, hg_ref, ga_ref, gh_ref,
                gw_ref, wa_ref, wh_ref, wm_ref, x1_ref):
    o = of_ref[...].astype(F32) + ob_ref[...].astype(F32)
    hg = hg_ref[...].astype(F32)
    gw = gw_ref[...]
    parts = []
    for hd in range(HGRN_HEADS):
        cols = slice(hd * HGRN_DK, (hd + 1) * HGRN_DK)
        oh = o[:, cols]
        ms = jnp.mean(oh * oh, axis=-1, keepdims=True)
        parts.append(oh * lax.rsqrt(ms + NORM_EPS))
    on = jnp.concatenate(parts, axis=1)
    rr = (on * gw * (hg * jax.nn.sigmoid(hg))).astype(BF16)
    pa = jnp.dot(a_ref[...], wa_ref[...], preferred_element_type=F32)
    ph = jnp.dot(rr, wh_ref[...], preferred_element_type=F32)
    mixed = (jax.nn.sigmoid(ga_ref[...].astype(F32)) * pa
             + jax.nn.sigmoid(gh_ref[...].astype(F32)) * ph).astype(BF16)
    x1_ref[...] = x_ref[...] + jnp.dot(mixed, wm_ref[...], preferred_element_type=F32)


def _mix(x2, a, of, ob, hg, ga, gh, gw, wa, wh, wm):
    n = x2.shape[0]
    tm = TOKEN_TILE
    row = lambda i: (i, 0)
    tok = lambda w: pl.BlockSpec((tm, w), row)
    return pl.pallas_call(
        _mix_kernel,
        grid=(n // tm,),
        in_specs=[tok(D_MODEL), tok(ATTN_WIDTH), tok(HGRN_WIDTH), tok(HGRN_WIDTH), tok(HGRN_WIDTH),
                  tok(D_MODEL), tok(D_MODEL),
                  _const_spec((1, HGRN_WIDTH)), _const_spec((ATTN_WIDTH, D_MODEL)),
                  _const_spec((HGRN_WIDTH, D_MODEL)), _const_spec((D_MODEL, D_MODEL))],
        out_specs=tok(D_MODEL),
        out_shape=jax.ShapeDtypeStruct((n, D_MODEL), F32),
        compiler_params=pltpu.CompilerParams(
            dimension_semantics=("arbitrary",), vmem_limit_bytes=VMEM_LIMIT_BYTES),
        name="mix",
    )(x2, a, of, ob, hg, ga, gh, gw, wa, wh, wm)


def _ffn_kernel(xp_ref, xc_ref, xn_ref, nw_ref, wu_ref, cw_ref, cb_ref, wd_ref, fw_ref,
                y_ref, hb_ref, hbf_ref, u_ref, act_ref, *, tiles_per_seq):
    i = pl.program_id(0)
    tm = xc_ref.shape[0]
    hl = SUBLANES
    nw = nw_ref[...]
    pos = i % tiles_per_seq
    keep_prev = jnp.where(pos != 0, 1.0, 0.0)
    keep_next = jnp.where(pos != tiles_per_seq - 1, 1.0, 0.0)
    hb_ref[0:hl, :] = _rms(xp_ref[...], nw) * keep_prev
    hb_ref[hl:hl + tm, :] = _rms(xc_ref[...], nw)
    hb_ref[hl + tm:, :] = _rms(xn_ref[...], nw) * keep_next
    hbf_ref[...] = hb_ref[...].astype(BF16)

    ck = FFN_COL_CHUNK
    nck = D_FF // ck
    nlb = ck // LANES

    def up(j):
        lo, hi = j * ck, (j + 1) * ck
        uv = jnp.dot(hbf_ref[...], wu_ref[:, lo:hi], preferred_element_type=F32)
        ug = jnp.dot(hbf_ref[...], wu_ref[:, D_FF + lo:D_FF + hi], preferred_element_type=F32)
        for k in range(nlb):
            u_ref[j, k] = uv[:, k * LANES:(k + 1) * LANES]
            u_ref[j, nlb + k] = ug[:, k * LANES:(k + 1) * LANES]

    def conv(j, k, col):
        cs = slice(col, col + LANES)
        return (cw_ref[0:1, cs] * u_ref[j, k, hl - 1:hl - 1 + tm, :]
                + cw_ref[1:2, cs] * u_ref[j, k, hl:hl + tm, :]
                + cw_ref[2:3, cs] * u_ref[j, k, hl + 1:hl + 1 + tm, :]
                + cb_ref[:, cs])

    def act(j):
        for k in range(nlb):
            col = j * ck + k * LANES
            val = conv(j, k, col)
            gate = conv(j, nlb + k, D_FF + col)
            act_ref[:, col:col + LANES] = (gate * jax.nn.sigmoid(gate) * val).astype(BF16)

    up(0)
    for j in range(nck):
        if j + 1 < nck:
            up(j + 1)
        act(j)
    y = xc_ref[...] + jnp.dot(act_ref[...], wd_ref[...], preferred_element_type=F32)
    y_ref[...] = _rms(y, fw_ref[...])


def _ffn(x1, nw, wu, cw, cb, wd, fw, seq_len):
    n = x1.shape[0]
    tm = TOKEN_TILE
    hl = SUBLANES
    per = tm // hl
    nhalo = n // hl
    row = lambda i: (i, 0)
    prev = lambda i: (jnp.maximum(i * per - 1, 0), 0)
    nxt = lambda i: (jnp.minimum((i + 1) * per, nhalo - 1), 0)
    kern = functools.partial(_ffn_kernel, tiles_per_seq=seq_len // tm)
    return pl.pallas_call(
        kern,
        grid=(n // tm,),
        in_specs=[pl.BlockSpec((hl, D_MODEL), prev), pl.BlockSpec((tm, D_MODEL), row),
                  pl.BlockSpec((hl, D_MODEL), nxt),
                  _const_spec((1, D_MODEL)), _const_spec((D_MODEL, 2 * D_FF)),
                  _const_spec((3, 2 * D_FF)), _const_spec((1, 2 * D_FF)),
                  _const_spec((D_FF, D_MODEL)), _const_spec((1, D_MODEL))],
        out_specs=pl.BlockSpec((tm, D_MODEL), row),
        out_shape=jax.ShapeDtypeStruct((n, D_MODEL), F32),
        scratch_shapes=[pltpu.VMEM((tm + 2 * hl, D_MODEL), F32),
                        pltpu.VMEM((tm + 2 * hl, D_MODEL), BF16),
                        pltpu.VMEM((D_FF // FFN_COL_CHUNK, 2 * FFN_COL_CHUNK // LANES, tm + 2 * hl, LANES), F32),
                        pltpu.VMEM((tm, D_FF), BF16)],
        compiler_params=pltpu.CompilerParams(
            dimension_semantics=("arbitrary",), vmem_limit_bytes=VMEM_LIMIT_BYTES),
        name="ffn",
    )(x1, x1, x1, nw, wu, cw, cb, wd, fw)


def _head_pair_perm():
    half = N_Q_HEADS // 2
    order = []
    for j in range(half):
        order += [j, j + half]
    return np.concatenate([np.arange(h * HEAD_DIM, (h + 1) * HEAD_DIM) for h in order])


def _rope_tables(seq_len):
    half = HEAD_DIM // 2
    inv_freq = ROPE_THETA ** (-jnp.arange(half, dtype=F32) / half)
    ang = jnp.arange(seq_len, dtype=F32)[:, None] * inv_freq[None, :]
    cos = jnp.cos(ang)
    sin = jnp.sin(ang)
    reps = LANES // HEAD_DIM
    cos_t = jnp.tile(jnp.concatenate([cos, cos], axis=1), (1, reps))
    sin_t = jnp.tile(jnp.concatenate([-sin, sin], axis=1), (1, reps))
    return cos_t, sin_t


def _trunk(x, p):
    batch, seq_len, _ = x.shape
    n = batch * seq_len
    x2 = x.reshape(n, D_MODEL)
    cos, sin = _rope_tables(seq_len)
    q, k, v, zh, hi, hg, ga, gh = _inproj(x2, p["norm_mix_w"], p["w_in"], cos, sin, seq_len)
    a = _attn(p["sink"], q, k, v, batch, seq_len)
    of, ob = _hgrn(zh, hi, p["lbl"], batch, seq_len)
    x1 = _mix(x2, a, of, ob, hg, ga, gh, p["hgrn_norm_w"], p["w_attn_out"], p["w_hgrn_out"], p["w_mix_out"])
    y = _ffn(x1, p["norm_ffn_w"], p["w_up"], p["conv_w"], p["conv_b"], p["w_down"], p["norm_final_w"], seq_len)
    return y.reshape(batch, seq_len, D_MODEL)


def kernel(x_prompt, x_sample, norm_mix_w, w_in, attn_sink, hgrn_lb_logits, hgrn_norm_w, w_attn_out, w_hgrn_out, w_mix_out, norm_ffn_w, w_up, conv_w, conv_b, w_down, norm_final_w):
    perm = _head_pair_perm()
    w_in0 = w_in[0]
    w_in_p = jnp.concatenate([w_in0[:, :ATTN_WIDTH][:, perm], w_in0[:, ATTN_WIDTH:]], axis=1)
    p = {
        "norm_mix_w": norm_mix_w[0].reshape(1, D_MODEL),
        "w_in": w_in_p.astype(BF16),
        "sink": attn_sink[0].astype(F32),
        "lbl": hgrn_lb_logits.astype(F32).reshape(hgrn_lb_logits.shape[0], 2 * HGRN_WIDTH),
        "hgrn_norm_w": hgrn_norm_w[0].astype(F32).reshape(1, HGRN_WIDTH),
        "w_attn_out": w_attn_out[0][perm, :].astype(BF16),
        "w_hgrn_out": w_hgrn_out[0].astype(BF16),
        "w_mix_out": w_mix_out[0].astype(BF16),
        "norm_ffn_w": norm_ffn_w[0].reshape(1, D_MODEL),
        "w_up": w_up[0].astype(BF16),
        "conv_w": conv_w[0],
        "conv_b": conv_b[0].reshape(1, 2 * D_FF),
        "w_down": w_down[0].astype(BF16),
        "norm_final_w": norm_final_w.reshape(1, D_MODEL),
    }
    return (_trunk(x_prompt, p), _trunk(x_sample, p))
```

```python
import functools

import jax
import jax.numpy as jnp
from jax import lax
from jax.experimental import pallas as pl
from jax.experimental.pallas import tpu as pltpu

D_MODEL = 1024
HEAD_DIM = 64
N_Q_HEADS = 8
N_KV_HEADS = 2
ATTN_WIDTH = N_Q_HEADS * HEAD_DIM
KV_WIDTH = N_KV_HEADS * HEAD_DIM
WINDOW = 128
ROPE_THETA = 10000.0
HGRN_HEADS = 4
HGRN_DK = 128
HGRN_WIDTH = HGRN_HEADS * HGRN_DK
HGRN_CHUNK = 64
D_FF = 2816
NORM_EPS = 1e-6

_OFF_Q = 0
_OFF_K = _OFF_Q + ATTN_WIDTH
_OFF_V = _OFF_K + KV_WIDTH
_OFF_HQ = _OFF_V + KV_WIDTH
_OFF_HFF = _OFF_HQ + HGRN_WIDTH
_OFF_HFB = _OFF_HFF + HGRN_WIDTH
_OFF_HI = _OFF_HFB + HGRN_WIDTH
_OFF_HG = _OFF_HI + HGRN_WIDTH
_OFF_GA = _OFF_HG + HGRN_WIDTH
_OFF_GH = _OFF_GA + D_MODEL
D_IN = _OFF_GH + D_MODEL

LANES = 128
SUBLANES = 8
VMEM_LIMIT_BYTES = 56 * 1024 * 1024

TOKEN_TILE = 512
ATTN_TILE = 512
FFN_COL_CHUNK = 256

BF16 = jnp.bfloat16
F32 = jnp.float32

_HEAD_PAIR_ORDER = tuple(h for j in range(N_Q_HEADS // 2) for h in (j, j + N_Q_HEADS // 2))


def _const_spec(shape):
    nd = len(shape)
    return pl.BlockSpec(shape, lambda *_: (0,) * nd, pipeline_mode=pl.Buffered(1))


def _rms(x, w):
    ms = jnp.mean(x * x, axis=-1, keepdims=True)
    return x * lax.rsqrt(ms + NORM_EPS) * w


def _rotate_half_pairs(x):
    lane = lax.broadcasted_iota(jnp.int32, x.shape, 1)
    first_half = (lane % HEAD_DIM) < (HEAD_DIM // 2)
    fwd = pltpu.roll(x, LANES - HEAD_DIM // 2, 1)
    bwd = pltpu.roll(x, HEAD_DIM // 2, 1)
    return jnp.where(first_half, fwd, bwd)


def _inproj_kernel(x_ref, nw_ref, w_ref, cos_ref, sin_ref,
                   q_ref, k_ref, v_ref, zh_ref, hi_ref, hg_ref, ga_ref, gh_ref, h_ref):
    h_ref[...] = _rms(x_ref[...], nw_ref[...]).astype(BF16)

    def proj(lo, hi):
        return jnp.dot(h_ref[...], w_ref[:, lo:hi], preferred_element_type=F32)

    def rope(z):
        return z * cos_ref[...] + _rotate_half_pairs(z) * sin_ref[...]

    zq = proj(_OFF_Q, _OFF_K)
    n_groups = ATTN_WIDTH // LANES
    grp = [rope(zq[:, g * LANES:(g + 1) * LANES]) * (HEAD_DIM ** -0.5) for g in range(n_groups)]
    lo_half = lax.broadcasted_iota(jnp.int32, grp[0].shape, 1) < HEAD_DIM
    swap = lambda z: pltpu.roll(z, HEAD_DIM, 1)
    for g in range(n_groups // 2):
        a, b = grp[g], grp[g + n_groups // 2]
        q_ref[:, 2 * g * LANES:(2 * g + 1) * LANES] = jnp.where(lo_half, a, swap(b)).astype(BF16)
        q_ref[:, (2 * g + 1) * LANES:(2 * g + 2) * LANES] = jnp.where(lo_half, swap(a), b).astype(BF16)
    k_ref[...] = rope(proj(_OFF_K, _OFF_V)).astype(BF16)
    v_ref[...] = proj(_OFF_V, _OFF_HQ).astype(BF16)
    zh_ref[...] = proj(_OFF_HQ, _OFF_HI)
    hi_ref[...] = proj(_OFF_HI, _OFF_HG).astype(BF16)
    hg_ref[...] = proj(_OFF_HG, _OFF_GA).astype(BF16)
    half = D_MODEL // 2
    for o_ref, off in ((ga_ref, _OFF_GA), (gh_ref, _OFF_GH)):
        for u in range(2):
            o_ref[:, u * half:(u + 1) * half] = proj(off + u * half, off + (u + 1) * half).astype(BF16)


def _inproj(x2, nw, w_in, cos, sin, seq_len):
    n = x2.shape[0]
    tm = TOKEN_TILE
    tiles_per_seq = seq_len // tm
    row = lambda i: (i, 0)
    pos = lambda i: (i % tiles_per_seq, 0)
    bf = lambda w: jax.ShapeDtypeStruct((n, w), BF16)
    tok = lambda w: pl.BlockSpec((tm, w), row)
    widths = (ATTN_WIDTH, KV_WIDTH, KV_WIDTH, 3 * HGRN_WIDTH, HGRN_WIDTH, HGRN_WIDTH, D_MODEL, D_MODEL)
    shapes = [bf(w) for w in widths]
    shapes[3] = jax.ShapeDtypeStruct((n, 3 * HGRN_WIDTH), F32)
    return pl.pallas_call(
        _inproj_kernel,
        grid=(n // tm,),
        in_specs=[tok(D_MODEL), _const_spec((1, D_MODEL)), _const_spec((D_MODEL, D_IN)),
                  pl.BlockSpec((tm, LANES), pos), pl.BlockSpec((tm, LANES), pos)],
        out_specs=[tok(w) for w in widths],
        out_shape=shapes,
        scratch_shapes=[pltpu.VMEM((tm, D_MODEL), BF16)],
        compiler_params=pltpu.CompilerParams(
            dimension_semantics=("arbitrary",), vmem_limit_bytes=VMEM_LIMIT_BYTES),
        name="inproj",
    )(x2, nw, w_in, cos, sin)


def _attn_kernel(sink_ref, q_ref, kp_ref, kc_ref, kn_ref, vp_ref, vc_ref, vn_ref, o_ref,
                 kw_ref, vw_ref, sc_ref):
    i = pl.program_id(1)
    nblk = pl.num_programs(1)
    tq = q_ref.shape[0]
    w = WINDOW
    kw_ref[0:w, :] = kp_ref[...]
    kw_ref[w:w + tq, :] = kc_ref[...]
    kw_ref[w + tq:, :] = kn_ref[...]
    vw_ref[0:w, :] = vp_ref[...]
    vw_ref[w:w + tq, :] = vc_ref[...]
    vw_ref[w + tq:, :] = vn_ref[...]

    lane = lax.broadcasted_iota(jnp.int32, (w, LANES), 1)
    lo_half = lane < HEAD_DIM
    lane_v = lax.broadcasted_iota(jnp.int32, (3 * w, LANES), 1)
    lo_half_v = lane_v < HEAD_DIM
    r = lax.broadcasted_iota(jnp.int32, (w, w), 0)
    cc = lax.broadcasted_iota(jnp.int32, (w, w), 1)
    band_prev = cc >= r
    band_next = cc <= r
    n_pairs = N_Q_HEADS // 2
    n_sub = tq // w

    for s in range(n_sub):
        ok_prev = band_prev if s > 0 else jnp.logical_and(band_prev, i != 0)
        ok_next = band_next if s < n_sub - 1 else jnp.logical_and(band_next, i != nblk - 1)
        kwin = kw_ref[s * w:s * w + 3 * w, :]
        vwin = vw_ref[s * w:s * w + 3 * w, :]
        zero_v = jnp.zeros_like(vwin)
        vblk = jnp.concatenate([jnp.where(lo_half_v, vwin, zero_v),
                                jnp.where(lo_half_v, zero_v, vwin)], axis=0)
        parts = []
        for j in range(n_pairs):
            qj = q_ref[s * w:(s + 1) * w, j * LANES:(j + 1) * LANES]
            zero_q = jnp.zeros_like(qj)
            parts.append(jnp.where(lo_half, qj, zero_q))
            parts.append(jnp.where(lo_half, zero_q, qj))
        lhs = jnp.concatenate(parts, axis=0)
        sc_ref[...] = lax.dot_general(lhs, kwin, (((1,), (1,)), ((), ())),
                                      preferred_element_type=F32)
        for j in range(n_pairs):
            pn = []
            inv = []
            for t in range(2):
                head = j + t * n_pairs
                rows = slice((2 * j + t) * w, (2 * j + t + 1) * w)
                s0 = jnp.where(ok_prev, sc_ref[rows, 0:w], -jnp.inf)
                s1 = sc_ref[rows, w:2 * w]
                s2 = jnp.where(ok_next, sc_ref[rows, 2 * w:3 * w], -jnp.inf)
                sink = sink_ref[head]
                m = jnp.max(jnp.maximum(jnp.maximum(s0, s1), s2), axis=-1, keepdims=True)
                m = jnp.maximum(m, sink)
                p0 = jnp.exp(s0 - m)
                p1 = jnp.exp(s1 - m)
                p2 = jnp.exp(s2 - m)
                denom = jnp.sum(p0 + p1 + p2, axis=-1, keepdims=True) + jnp.exp(sink - m)
                inv.append(1.0 / denom)
                pn += [p0.astype(BF16), p1.astype(BF16), p2.astype(BF16)]
            pcat = jnp.concatenate(pn, axis=1)
            o = jnp.dot(pcat, vblk, preferred_element_type=F32)
            o = o * jnp.where(lo_half, inv[0], inv[1])
            o_ref[s * w:(s + 1) * w, j * LANES:(j + 1) * LANES] = o.astype(BF16)


def _attn(sink, q, k, v, batch, seq_len):
    tq = ATTN_TILE
    w = WINDOW
    nblk = seq_len // tq
    sub = tq // w
    nrow_blocks = batch * seq_len // w
    cur = lambda b, i, *_: (b * nblk + i, 0)
    prev = lambda b, i, *_: (jnp.maximum((b * nblk + i) * sub - 1, 0), 0)
    nxt = lambda b, i, *_: (jnp.minimum((b * nblk + i + 1) * sub, nrow_blocks - 1), 0)
    kv_cur = pl.BlockSpec((tq, KV_WIDTH), cur)
    kv_prev = pl.BlockSpec((w, KV_WIDTH), prev)
    kv_next = pl.BlockSpec((w, KV_WIDTH), nxt)
    return pl.pallas_call(
        _attn_kernel,
        grid_spec=pltpu.PrefetchScalarGridSpec(
            num_scalar_prefetch=1,
            grid=(batch, nblk),
            in_specs=[pl.BlockSpec((tq, ATTN_WIDTH), cur),
                      kv_prev, kv_cur, kv_next, kv_prev, kv_cur, kv_next],
            out_specs=pl.BlockSpec((tq, ATTN_WIDTH), cur),
            scratch_shapes=[pltpu.VMEM((tq + 2 * w, KV_WIDTH), BF16),
                            pltpu.VMEM((tq + 2 * w, KV_WIDTH), BF16),
                            pltpu.VMEM((N_Q_HEADS * w, 3 * w), F32)],
        ),
        out_shape=jax.ShapeDtypeStruct(q.shape, BF16),
        compiler_params=pltpu.CompilerParams(
            dimension_semantics=("arbitrary", "arbitrary"), vmem_limit_bytes=VMEM_LIMIT_BYTES),
        name="attn",
    )(sink, q, k, k, k, v, v, v)


def _hgrn_kernel(hq_f_ref, xf_f_ref, v_f_ref, hq_b_ref, xf_b_ref, v_b_ref, lbl_ref,
                 of_ref, ob_ref,
                 s_ref, sbf_ref, p_ref, dc_ref, a_ref, u_ref, gp_ref, kk_ref, b_ref):
    @pl.when(pl.program_id(1) == 0)
    def _():
        s_ref[...] = jnp.zeros_like(s_ref)
        sbf_ref[...] = jnp.zeros_like(sbf_ref)

    c = HGRN_CHUNK
    n_chunks = of_ref.shape[0] // c
    r = lax.broadcasted_iota(jnp.int32, (c, c), 0)
    cc = lax.broadcasted_iota(jnp.int32, (c, c), 1)
    keep = (r >= cc, r <= cc)
    tri = (keep[0].astype(BF16), keep[1].astype(BF16))
    nt = (((1,), (1,)), ((), ()))
    tn = (((0,), (0,)), ((), ()))
    ins = ((hq_f_ref, xf_f_ref, v_f_ref, of_ref), (hq_b_ref, xf_b_ref, v_b_ref, ob_ref))
    n_strips = HGRN_WIDTH // LANES

    lbl = lbl_ref[...]
    e = jnp.exp(lbl - jnp.max(lbl, axis=0, keepdims=True))
    lb = e[0:1, :] / jnp.sum(e, axis=0, keepdims=True)

    def chunk_rows(d, t):
        ci = t if d == 0 else n_chunks - 1 - t
        return slice(ci * c, (ci + 1) * c)

    def stage1(t, d, blk):
        rows = chunk_rows(d, t)
        cols = slice(blk * LANES, (blk + 1) * LANES)
        wide = slice(d * HGRN_WIDTH + blk * LANES, d * HGRN_WIDTH + (blk + 1) * LANES)
        xf = ins[d][1][rows, cols]
        lbd = lb[:, wide]
        f = lbd + (1.0 - lbd) * jax.nn.sigmoid(xf)
        kk_ref[:, wide] = 1.0 - f
        g = jnp.log(f)
        g1 = g.astype(BF16)
        r1 = g - g1.astype(F32)
        g2 = r1.astype(BF16)
        gp_ref[0, :, wide] = g1
        gp_ref[1, :, wide] = g2
        gp_ref[2, :, wide] = (r1 - g2.astype(F32)).astype(BF16)

    def cumsum(d):
        wide = slice(d * HGRN_WIDTH, (d + 1) * HGRN_WIDTH)
        b_ref[:, wide] = (jnp.dot(tri[d], gp_ref[0, :, wide], preferred_element_type=F32)
                          + jnp.dot(tri[d], gp_ref[1, :, wide], preferred_element_type=F32)
                          + jnp.dot(tri[d], gp_ref[2, :, wide], preferred_element_type=F32))

    def stage3(slot, t, d, blk):
        rows = chunk_rows(d, t)
        cols = slice(blk * LANES, (blk + 1) * LANES)
        wide = slice(d * HGRN_WIDTH + blk * LANES, d * HGRN_WIDTH + (blk + 1) * LANES)
        hq = ins[d][0][rows, cols]
        b = b_ref[:, wide]
        if d == 0:
            ref = b[c // 2 - 1:c // 2, :]
            b_last = b[c - 1:c, :]
        else:
            ref = b[c // 2:c // 2 + 1, :]
            b_last = b[0:1, :]
        qd = hq * jnp.exp(b - ref)
        kd = kk_ref[:, wide] * jnp.exp(ref - b)
        p_ref[slot, d, 0, :, cols] = qd.astype(BF16)
        p_ref[slot, d, 1, :, cols] = kd.astype(BF16)
        p_ref[slot, d, 2, :, cols] = (qd * jnp.exp(ref)).astype(BF16)
        p_ref[slot, d, 3, :, cols] = (kd * jnp.exp(b_last - ref)).astype(BF16)
        dc_ref[slot, d, 0:1, cols] = jnp.exp(b_last)

    units = [(d, hd) for d in range(2) for hd in range(HGRN_HEADS)]
    strips = [(d, blk) for blk in range(n_strips) for d in range(2)]

    def step(slot, t, prep_next):
        if prep_next:
            for d, blk in strips:
                stage1(t + 1, d, blk)
        for d, hd in units:
            cols = slice(hd * HGRN_DK, (hd + 1) * HGRN_DK)
            a = lax.dot_general(p_ref[slot, d, 0, :, cols], p_ref[slot, d, 1, :, cols], nt,
                                preferred_element_type=F32)
            a_ref[d * HGRN_HEADS + hd] = jnp.where(keep[d], a, 0.0).astype(BF16)
        if prep_next:
            cumsum(0)
            cumsum(1)
        for d, hd in units:
            cols = slice(hd * HGRN_DK, (hd + 1) * HGRN_DK)
            vh = ins[d][2][chunk_rows(d, t), cols]
            u_ref[d * HGRN_HEADS + hd] = lax.dot_general(vh, p_ref[slot, d, 3, :, cols], tn,
                                                         preferred_element_type=F32)
        for d, hd in units:
            cols = slice(hd * HGRN_DK, (hd + 1) * HGRN_DK)
            idx = d * HGRN_HEADS + hd
            rows = chunk_rows(d, t)
            o = lax.dot_general(p_ref[slot, d, 2, :, cols], sbf_ref[idx], nt, preferred_element_type=F32)
            o = o + jnp.dot(a_ref[idx], ins[d][2][rows, cols], preferred_element_type=F32)
            ins[d][3][rows, cols] = o.astype(ins[d][3].dtype)
        if prep_next:
            for d, blk in strips:
                stage3(1 - slot, t + 1, d, blk)
        for d, hd in units:
            cols = slice(hd * HGRN_DK, (hd + 1) * HGRN_DK)
            idx = d * HGRN_HEADS + hd
            st = s_ref[idx] * dc_ref[slot, d, 0:1, cols] + u_ref[idx]
            s_ref[idx] = st
            sbf_ref[idx] = st.astype(BF16)

    for d, blk in strips:
        stage1(0, d, blk)
    cumsum(0)
    cumsum(1)
    for d, blk in strips:
        stage3(0, 0, d, blk)
    for t in range(n_chunks):
        step(t % 2, t, t + 1 < n_chunks)


def _hgrn(zh, v, lbl, batch, seq_len):
    tm = TOKEN_TILE
    nblk = seq_len // tm
    n = v.shape[0]
    w = HGRN_WIDTH
    fwd = lambda col: (lambda b, i: (b * nblk + i, col))
    bwd = lambda col: (lambda b, i: (b * nblk + nblk - 1 - i, col))
    blk = lambda imap: pl.BlockSpec((tm, w), imap)
    out = jax.ShapeDtypeStruct((n, w), BF16)
    nu = 2 * HGRN_HEADS
    return pl.pallas_call(
        _hgrn_kernel,
        grid=(batch, nblk),
        in_specs=[blk(fwd(0)), blk(fwd(1)), blk(fwd(0)), blk(bwd(0)), blk(bwd(2)), blk(bwd(0)),
                  _const_spec(lbl.shape)],
        out_specs=[blk(fwd(0)), blk(bwd(0))],
        out_shape=[out, out],
        scratch_shapes=[pltpu.VMEM((nu, HGRN_DK, HGRN_DK), F32),
                        pltpu.VMEM((nu, HGRN_DK, HGRN_DK), BF16),
                        pltpu.VMEM((2, 2, 4, HGRN_CHUNK, w), BF16),
                        pltpu.VMEM((2, 2, SUBLANES, w), F32),
                        pltpu.VMEM((nu, HGRN_CHUNK, HGRN_CHUNK), BF16),
                        pltpu.VMEM((nu, HGRN_DK, HGRN_DK), F32),
                        pltpu.VMEM((3, HGRN_CHUNK, 2 * w), BF16),
                        pltpu.VMEM((HGRN_CHUNK, 2 * w), F32),
                        pltpu.VMEM((HGRN_CHUNK, 2 * w), F32)],
        compiler_params=pltpu.CompilerParams(
            dimension_semantics=("arbitrary", "arbitrary"), vmem_limit_bytes=VMEM_LIMIT_BYTES),
        name="hgrn",
    )(zh, zh, v, zh, zh, v, lbl)


def _mix_kernel(x_ref, a_ref, of_ref, ob_ref, hg_ref, ga_ref, gh_ref,
                gw_ref, wa_ref, wh_ref, wm_ref, x1_ref):
    o = of_ref[...].astype(F32) + ob_ref[...].astype(F32)
    hg = hg_ref[...].astype(F32)
    gw = gw_ref[...]
    parts = []
    for hd in range(HGRN_HEADS):
        cols = slice(hd * HGRN_DK, (hd + 1) * HGRN_DK)
        oh = o[:, cols]
        ms = jnp.mean(oh * oh, axis=-1, keepdims=True)
        parts.append(oh * lax.rsqrt(ms + NORM_EPS))
    on = jnp.concatenate(parts, axis=1)
    rr = (on * gw * (hg * jax.nn.sigmoid(hg))).astype(BF16)
    wa = jnp.concatenate([wa_ref[h * HEAD_DIM:(h + 1) * HEAD_DIM, :] for h in _HEAD_PAIR_ORDER], axis=0)
    pa = jnp.dot(a_ref[...], wa, preferred_element_type=F32)
    ph = jnp.dot(rr, wh_ref[...], preferred_element_type=F32)
    mixed = (jax.nn.sigmoid(ga_ref[...].astype(F32)) * pa
             + jax.nn.sigmoid(gh_ref[...].astype(F32)) * ph).astype(BF16)
    x1_ref[...] = x_ref[...] + jnp.dot(mixed, wm_ref[...], preferred_element_type=F32)


def _mix(x2, a, of, ob, hg, ga, gh, gw, wa, wh, wm):
    n = x2.shape[0]
    tm = TOKEN_TILE
    row = lambda i: (i, 0)
    tok = lambda w: pl.BlockSpec((tm, w), row)
    return pl.pallas_call(
        _mix_kernel,
        grid=(n // tm,),
        in_specs=[tok(D_MODEL), tok(ATTN_WIDTH), tok(HGRN_WIDTH), tok(HGRN_WIDTH), tok(HGRN_WIDTH),
                  tok(D_MODEL), tok(D_MODEL),
                  _const_spec((1, HGRN_WIDTH)), _const_spec((ATTN_WIDTH, D_MODEL)),
                  _const_spec((HGRN_WIDTH, D_MODEL)), _const_spec((D_MODEL, D_MODEL))],
        out_specs=tok(D_MODEL),
        out_shape=jax.ShapeDtypeStruct((n, D_MODEL), F32),
        compiler_params=pltpu.CompilerParams(
            dimension_semantics=("arbitrary",), vmem_limit_bytes=VMEM_LIMIT_BYTES),
        name="mix",
    )(x2, a, of, ob, hg, ga, gh, gw, wa, wh, wm)


def _ffn_kernel(xp_ref, xc_ref, xn_ref, nw_ref, wu_ref, cw_ref, cb_ref, wd_ref, fw_ref,
                y_ref, hb_ref, hbf_ref, u_ref, act_ref, *, tiles_per_seq):
    i = pl.program_id(0)
    tm = xc_ref.shape[0]
    hl = SUBLANES
    nw = nw_ref[...]
    pos = i % tiles_per_seq
    keep_prev = jnp.where(pos != 0, 1.0, 0.0)
    keep_next = jnp.where(pos != tiles_per_seq - 1, 1.0, 0.0)
    hb_ref[0:hl, :] = _rms(xp_ref[...], nw) * keep_prev
    hb_ref[hl:hl + tm, :] = _rms(xc_ref[...], nw)
    hb_ref[hl + tm:, :] = _rms(xn_ref[...], nw) * keep_next
    hbf_ref[...] = hb_ref[...].astype(BF16)

    ck = FFN_COL_CHUNK
    nck = D_FF // ck
    nlb = ck // LANES

    def up(j):
        lo, hi = j * ck, (j + 1) * ck
        uv = jnp.dot(hbf_ref[...], wu_ref[:, lo:hi], preferred_element_type=F32)
        ug = jnp.dot(hbf_ref[...], wu_ref[:, D_FF + lo:D_FF + hi], preferred_element_type=F32)
        for k in range(nlb):
            u_ref[j, k] = uv[:, k * LANES:(k + 1) * LANES]
            u_ref[j, nlb + k] = ug[:, k * LANES:(k + 1) * LANES]

    def conv(j, k, col):
        cs = slice(col, col + LANES)
        return (cw_ref[0:1, cs] * u_ref[j, k, hl - 1:hl - 1 + tm, :]
                + cw_ref[1:2, cs] * u_ref[j, k, hl:hl + tm, :]
                + cw_ref[2:3, cs] * u_ref[j, k, hl + 1:hl + 1 + tm, :]
                + cb_ref[:, cs])

    def act(j):
        for k in range(nlb):
            col = j * ck + k * LANES
            val = conv(j, k, col)
            gate = conv(j, nlb + k, D_FF + col)
            act_ref[:, col:col + LANES] = (gate * jax.nn.sigmoid(gate) * val).astype(BF16)

    up(0)
    for j in range(nck):
        if j + 1 < nck:
            up(j + 1)
        act(j)
    y = xc_ref[...] + jnp.dot(act_ref[...], wd_ref[...], preferred_element_type=F32)
    y_ref[...] = _rms(y, fw_ref[...])


def _ffn(x1, nw, wu, cw, cb, wd, fw, seq_len):
    n = x1.shape[0]
    tm = TOKEN_TILE
    hl = SUBLANES
    per = tm // hl
    nhalo = n // hl
    row = lambda i: (i, 0)
    prev = lambda i: (jnp.maximum(i * per - 1, 0), 0)
    nxt = lambda i: (jnp.minimum((i + 1) * per, nhalo - 1), 0)
    kern = functools.partial(_ffn_kernel, tiles_per_seq=seq_len // tm)
    return pl.pallas_call(
        kern,
        grid=(n // tm,),
        in_specs=[pl.BlockSpec((hl, D_MODEL), prev), pl.BlockSpec((tm, D_MODEL), row),
                  pl.BlockSpec((hl, D_MODEL), nxt),
                  _const_spec((1, D_MODEL)), _const_spec((D_MODEL, 2 * D_FF)),
                  _const_spec((3, 2 * D_FF)), _const_spec((1, 2 * D_FF)),
                  _const_spec((D_FF, D_MODEL)), _const_spec((1, D_MODEL))],
        out_specs=pl.BlockSpec((tm, D_MODEL), row),
        out_shape=jax.ShapeDtypeStruct((n, D_MODEL), F32),
        scratch_shapes=[pltpu.VMEM((tm + 2 * hl, D_MODEL), F32),
                        pltpu.VMEM((tm + 2 * hl, D_MODEL), BF16),
                        pltpu.VMEM((D_FF // FFN_COL_CHUNK, 2 * FFN_COL_CHUNK // LANES, tm + 2 * hl, LANES), F32),
                        pltpu.VMEM((tm, D_FF), BF16)],
        compiler_params=pltpu.CompilerParams(
            dimension_semantics=("arbitrary",), vmem_limit_bytes=VMEM_LIMIT_BYTES),
        name="ffn",
    )(x1, x1, x1, nw, wu, cw, cb, wd, fw)


def _rope_tables(seq_len):
    half = HEAD_DIM // 2
    inv_freq = ROPE_THETA ** (-jnp.arange(half, dtype=F32) / half)
    ang = jnp.arange(seq_len, dtype=F32)[:, None] * inv_freq[None, :]
    cos = jnp.cos(ang)
    sin = jnp.sin(ang)
    reps = LANES // HEAD_DIM
    cos_t = jnp.tile(jnp.concatenate([cos, cos], axis=1), (1, reps))
    sin_t = jnp.tile(jnp.concatenate([-sin, sin], axis=1), (1, reps))
    return cos_t, sin_t


def _trunk(x, p):
    batch, seq_len, _ = x.shape
    n = batch * seq_len
    x2 = x.reshape(n, D_MODEL)
    cos, sin = p["rope"][0][:seq_len], p["rope"][1][:seq_len]
    q, k, v, zh, hi, hg, ga, gh = _inproj(x2, p["norm_mix_w"], p["w_in"], cos, sin, seq_len)
    a = _attn(p["sink"], q, k, v, batch, seq_len)
    of, ob = _hgrn(zh, hi, p["lbl"], batch, seq_len)
    x1 = _mix(x2, a, of, ob, hg, ga, gh, p["hgrn_norm_w"], p["w_attn_out"], p["w_hgrn_out"], p["w_mix_out"])
    y = _ffn(x1, p["norm_ffn_w"], p["w_up"], p["conv_w"], p["conv_b"], p["w_down"], p["norm_final_w"], seq_len)
    return y.reshape(batch, seq_len, D_MODEL)


def kernel(x_prompt, x_sample, norm_mix_w, w_in, attn_sink, hgrn_lb_logits, hgrn_norm_w, w_attn_out, w_hgrn_out, w_mix_out, norm_ffn_w, w_up, conv_w, conv_b, w_down, norm_final_w):
    p = {
        "rope": _rope_tables(max(x_prompt.shape[1], x_sample.shape[1])),
        "norm_mix_w": norm_mix_w[0].reshape(1, D_MODEL),
        "w_in": w_in[0].astype(BF16),
        "sink": attn_sink[0].astype(F32),
        "lbl": hgrn_lb_logits.astype(F32).reshape(hgrn_lb_logits.shape[0], 2 * HGRN_WIDTH),
        "hgrn_norm_w": hgrn_norm_w[0].astype(F32).reshape(1, HGRN_WIDTH),
        "w_attn_out": w_attn_out[0].astype(BF16),
        "w_hgrn_out": w_hgrn_out[0].astype(BF16),
        "w_mix_out": w_mix_out[0].astype(BF16),
        "norm_ffn_w": norm_ffn_w[0].reshape(1, D_MODEL),
        "w_up": w_up[0].astype(BF16),
        "conv_w": conv_w[0],
        "conv_b": conv_b[0].reshape(1, 2 * D_FF),
        "w_down": w_down[0].astype(BF16),
        "norm_final_w": norm_final_w.reshape(1, D_MODEL),
    }
    return (_trunk(x_prompt, p), _trunk(x_sample, p))
```

```python
import functools

import jax
import jax.numpy as jnp
from jax import lax
from jax.experimental import pallas as pl
from jax.experimental.pallas import tpu as pltpu

D_MODEL = 1024
HEAD_DIM = 64
N_Q_HEADS = 8
N_KV_HEADS = 2
ATTN_WIDTH = N_Q_HEADS * HEAD_DIM
KV_WIDTH = N_KV_HEADS * HEAD_DIM
WINDOW = 128
ROPE_THETA = 10000.0
HGRN_HEADS = 4
HGRN_DK = 128
HGRN_WIDTH = HGRN_HEADS * HGRN_DK
HGRN_CHUNK = 64
D_FF = 2816
NORM_EPS = 1e-6

_OFF_Q = 0
_OFF_K = _OFF_Q + ATTN_WIDTH
_OFF_V = _OFF_K + KV_WIDTH
_OFF_HQ = _OFF_V + KV_WIDTH
_OFF_HFF = _OFF_HQ + HGRN_WIDTH
_OFF_HFB = _OFF_HFF + HGRN_WIDTH
_OFF_HI = _OFF_HFB + HGRN_WIDTH
_OFF_HG = _OFF_HI + HGRN_WIDTH
_OFF_GA = _OFF_HG + HGRN_WIDTH
_OFF_GH = _OFF_GA + D_MODEL
D_IN = _OFF_GH + D_MODEL

LANES = 128
SUBLANES = 8
VMEM_LIMIT_BYTES = 56 * 1024 * 1024

TOKEN_TILE = 512
ATTN_TILE = 512
FFN_COL_CHUNK = 256

BF16 = jnp.bfloat16
F32 = jnp.float32

_HEAD_PAIR_ORDER = tuple(h for j in range(N_Q_HEADS // 2) for h in (j, j + N_Q_HEADS // 2))


def _const_spec(shape):
    nd = len(shape)
    return pl.BlockSpec(shape, lambda *_: (0,) * nd, pipeline_mode=pl.Buffered(1))


def _rms(x, w):
    ms = jnp.mean(x * x, axis=-1, keepdims=True)
    return x * lax.rsqrt(ms + NORM_EPS) * w


def _rotate_half_pairs(x):
    lane = lax.broadcasted_iota(jnp.int32, x.shape, 1)
    first_half = (lane % HEAD_DIM) < (HEAD_DIM // 2)
    fwd = pltpu.roll(x, LANES - HEAD_DIM // 2, 1)
    bwd = pltpu.roll(x, HEAD_DIM // 2, 1)
    return jnp.where(first_half, fwd, bwd)


def _inproj_kernel(x_ref, nw_ref, w_ref, cos_ref, sin_ref,
                   q_ref, k_ref, v_ref, hq_ref, xff_ref, xfb_ref, hi_ref, h_ref):
    h_ref[...] = _rms(x_ref[...], nw_ref[...]).astype(BF16)

    def proj(lo, hi):
        return jnp.dot(h_ref[...], w_ref[:, lo:hi], preferred_element_type=F32)

    def rope(z):
        return z * cos_ref[...] + _rotate_half_pairs(z) * sin_ref[...]

    zq = proj(_OFF_Q, _OFF_K)
    n_groups = ATTN_WIDTH // LANES
    grp = [rope(zq[:, g * LANES:(g + 1) * LANES]) * (HEAD_DIM ** -0.5) for g in range(n_groups)]
    lo_half = lax.broadcasted_iota(jnp.int32, grp[0].shape, 1) < HEAD_DIM
    swap = lambda z: pltpu.roll(z, HEAD_DIM, 1)
    for g in range(n_groups // 2):
        a, b = grp[g], grp[g + n_groups // 2]
        q_ref[:, 2 * g * LANES:(2 * g + 1) * LANES] = jnp.where(lo_half, a, swap(b)).astype(BF16)
        q_ref[:, (2 * g + 1) * LANES:(2 * g + 2) * LANES] = jnp.where(lo_half, swap(a), b).astype(BF16)
    k_ref[...] = rope(proj(_OFF_K, _OFF_V)).astype(BF16)
    v_ref[...] = proj(_OFF_V, _OFF_HQ).astype(BF16)
    hq_ref[...] = proj(_OFF_HQ, _OFF_HFF)
    xff_ref[...] = proj(_OFF_HFF, _OFF_HFB)
    xfb_ref[...] = proj(_OFF_HFB, _OFF_HI)
    hi_ref[...] = proj(_OFF_HI, _OFF_HG).astype(BF16)


def _inproj(x2, nw, w_in, cos, sin, seq_len):
    n = x2.shape[0]
    tm = TOKEN_TILE
    tiles_per_seq = seq_len // tm
    row = lambda i: (i, 0)
    pos = lambda i: (i % tiles_per_seq, 0)
    tok = lambda w: pl.BlockSpec((tm, w), row)
    widths = (ATTN_WIDTH, KV_WIDTH, KV_WIDTH, HGRN_WIDTH, HGRN_WIDTH, HGRN_WIDTH, HGRN_WIDTH)
    dtypes = (BF16, BF16, BF16, F32, F32, F32, BF16)
    shapes = [jax.ShapeDtypeStruct((n, w), dt) for w, dt in zip(widths, dtypes)]
    return pl.pallas_call(
        _inproj_kernel,
        grid=(n // tm,),
        in_specs=[tok(D_MODEL), _const_spec((1, D_MODEL)), _const_spec(w_in.shape),
                  pl.BlockSpec((tm, LANES), pos), pl.BlockSpec((tm, LANES), pos)],
        out_specs=[tok(w) for w in widths],
        out_shape=shapes,
        scratch_shapes=[pltpu.VMEM((tm, D_MODEL), BF16)],
        compiler_params=pltpu.CompilerParams(
            dimension_semantics=("arbitrary",), vmem_limit_bytes=VMEM_LIMIT_BYTES),
        name="inproj",
    )(x2, nw, w_in, cos, sin)


def _attn_kernel(sink_ref, q_ref, kp_ref, kc_ref, kn_ref, vp_ref, vc_ref, vn_ref, o_ref,
                 kw_ref, vw_ref, sc_ref):
    i = pl.program_id(1)
    nblk = pl.num_programs(1)
    tq = q_ref.shape[0]
    w = WINDOW
    kw_ref[0:w, :] = kp_ref[...]
    kw_ref[w:w + tq, :] = kc_ref[...]
    kw_ref[w + tq:, :] = kn_ref[...]
    vw_ref[0:w, :] = vp_ref[...]
    vw_ref[w:w + tq, :] = vc_ref[...]
    vw_ref[w + tq:, :] = vn_ref[...]

    lane = lax.broadcasted_iota(jnp.int32, (w, LANES), 1)
    lo_half = lane < HEAD_DIM
    lane_v = lax.broadcasted_iota(jnp.int32, (3 * w, LANES), 1)
    lo_half_v = lane_v < HEAD_DIM
    r = lax.broadcasted_iota(jnp.int32, (w, w), 0)
    cc = lax.broadcasted_iota(jnp.int32, (w, w), 1)
    band_prev = cc >= r
    band_next = cc <= r
    n_pairs = N_Q_HEADS // 2
    n_sub = tq // w

    for s in range(n_sub):
        ok_prev = band_prev if s > 0 else jnp.logical_and(band_prev, i != 0)
        ok_next = band_next if s < n_sub - 1 else jnp.logical_and(band_next, i != nblk - 1)
        kwin = kw_ref[s * w:s * w + 3 * w, :]
        vwin = vw_ref[s * w:s * w + 3 * w, :]
        zero_v = jnp.zeros_like(vwin)
        vblk = jnp.concatenate([jnp.where(lo_half_v, vwin, zero_v),
                                jnp.where(lo_half_v, zero_v, vwin)], axis=0)
        parts = []
        for j in range(n_pairs):
            qj = q_ref[s * w:(s + 1) * w, j * LANES:(j + 1) * LANES]
            zero_q = jnp.zeros_like(qj)
            parts.append(jnp.where(lo_half, qj, zero_q))
            parts.append(jnp.where(lo_half, zero_q, qj))
        lhs = jnp.concatenate(parts, axis=0)
        sc_ref[...] = lax.dot_general(lhs, kwin, (((1,), (1,)), ((), ())),
                                      preferred_element_type=F32)
        for j in range(n_pairs):
            pn = []
            inv = []
            for t in range(2):
                head = j + t * n_pairs
                rows = slice((2 * j + t) * w, (2 * j + t + 1) * w)
                s0 = jnp.where(ok_prev, sc_ref[rows, 0:w], -jnp.inf)
                s1 = sc_ref[rows, w:2 * w]
                s2 = jnp.where(ok_next, sc_ref[rows, 2 * w:3 * w], -jnp.inf)
                sink = sink_ref[head]
                m = jnp.max(jnp.maximum(jnp.maximum(s0, s1), s2), axis=-1, keepdims=True)
                m = jnp.maximum(m, sink)
                p0 = jnp.exp(s0 - m)
                p1 = jnp.exp(s1 - m)
                p2 = jnp.exp(s2 - m)
                denom = jnp.sum(p0 + p1 + p2, axis=-1, keepdims=True) + jnp.exp(sink - m)
                inv.append(1.0 / denom)
                pn += [p0.astype(BF16), p1.astype(BF16), p2.astype(BF16)]
            pcat = jnp.concatenate(pn, axis=1)
            o = jnp.dot(pcat, vblk, preferred_element_type=F32)
            o = o * jnp.where(lo_half, inv[0], inv[1])
            o_ref[s * w:(s + 1) * w, j * LANES:(j + 1) * LANES] = o.astype(BF16)


def _attn(sink, q, k, v, batch, seq_len):
    tq = ATTN_TILE
    w = WINDOW
    nblk = seq_len // tq
    sub = tq // w
    nrow_blocks = batch * seq_len // w
    cur = lambda b, i, *_: (b * nblk + i, 0)
    prev = lambda b, i, *_: (jnp.maximum((b * nblk + i) * sub - 1, 0), 0)
    nxt = lambda b, i, *_: (jnp.minimum((b * nblk + i + 1) * sub, nrow_blocks - 1), 0)
    kv_cur = pl.BlockSpec((tq, KV_WIDTH), cur)
    kv_prev = pl.BlockSpec((w, KV_WIDTH), prev)
    kv_next = pl.BlockSpec((w, KV_WIDTH), nxt)
    return pl.pallas_call(
        _attn_kernel,
        grid_spec=pltpu.PrefetchScalarGridSpec(
            num_scalar_prefetch=1,
            grid=(batch, nblk),
            in_specs=[pl.BlockSpec((tq, ATTN_WIDTH), cur),
                      kv_prev, kv_cur, kv_next, kv_prev, kv_cur, kv_next],
            out_specs=pl.BlockSpec((tq, ATTN_WIDTH), cur),
            scratch_shapes=[pltpu.VMEM((tq + 2 * w, KV_WIDTH), BF16),
                            pltpu.VMEM((tq + 2 * w, KV_WIDTH), BF16),
                            pltpu.VMEM((N_Q_HEADS * w, 3 * w), F32)],
        ),
        out_shape=jax.ShapeDtypeStruct(q.shape, BF16),
        compiler_params=pltpu.CompilerParams(
            dimension_semantics=("arbitrary", "arbitrary"), vmem_limit_bytes=VMEM_LIMIT_BYTES),
        name="attn",
    )(sink, q, k, k, k, v, v, v)


def _hgrn_kernel(hq_f_ref, xf_f_ref, v_f_ref, hq_b_ref, xf_b_ref, v_b_ref, lbl_ref,
                 of_ref, ob_ref,
                 s_ref, sbf_ref, p_ref, dc_ref, a_ref, u_ref, gp_ref, kk_ref, b_ref):
    @pl.when(pl.program_id(1) == 0)
    def _():
        s_ref[...] = jnp.zeros_like(s_ref)
        sbf_ref[...] = jnp.zeros_like(sbf_ref)

    c = HGRN_CHUNK
    n_chunks = of_ref.shape[0] // c
    r = lax.broadcasted_iota(jnp.int32, (c, c), 0)
    cc = lax.broadcasted_iota(jnp.int32, (c, c), 1)
    keep = (r >= cc, r <= cc)
    tri = (keep[0].astype(BF16), keep[1].astype(BF16))
    nt = (((1,), (1,)), ((), ()))
    tn = (((0,), (0,)), ((), ()))
    ins = ((hq_f_ref, xf_f_ref, v_f_ref, of_ref), (hq_b_ref, xf_b_ref, v_b_ref, ob_ref))
    n_strips = HGRN_WIDTH // LANES

    lbl = lbl_ref[...]
    e = jnp.exp(lbl - jnp.max(lbl, axis=0, keepdims=True))
    lb = e[0:1, :] / jnp.sum(e, axis=0, keepdims=True)

    def chunk_rows(d, t):
        ci = t if d == 0 else n_chunks - 1 - t
        return slice(ci * c, (ci + 1) * c)

    def stage1(t, d, blk):
        rows = chunk_rows(d, t)
        cols = slice(blk * LANES, (blk + 1) * LANES)
        wide = slice(d * HGRN_WIDTH + blk * LANES, d * HGRN_WIDTH + (blk + 1) * LANES)
        xf = ins[d][1][rows, cols]
        lbd = lb[:, wide]
        f = lbd + (1.0 - lbd) * jax.nn.sigmoid(xf)
        kk_ref[:, wide] = 1.0 - f
        g = jnp.log(f)
        g1 = g.astype(BF16)
        r1 = g - g1.astype(F32)
        g2 = r1.astype(BF16)
        gp_ref[0, :, wide] = g1
        gp_ref[1, :, wide] = g2
        gp_ref[2, :, wide] = (r1 - g2.astype(F32)).astype(BF16)

    def cumsum(d):
        wide = slice(d * HGRN_WIDTH, (d + 1) * HGRN_WIDTH)
        b_ref[:, wide] = (jnp.dot(tri[d], gp_ref[0, :, wide], preferred_element_type=F32)
                          + jnp.dot(tri[d], gp_ref[1, :, wide], preferred_element_type=F32)
                          + jnp.dot(tri[d], gp_ref[2, :, wide], preferred_element_type=F32))

    def stage3(slot, t, d, blk):
        rows = chunk_rows(d, t)
        cols = slice(blk * LANES, (blk + 1) * LANES)
        wide = slice(d * HGRN_WIDTH + blk * LANES, d * HGRN_WIDTH + (blk + 1) * LANES)
        hq = ins[d][0][rows, cols]
        b = b_ref[:, wide]
        if d == 0:
            ref = b[c // 2 - 1:c // 2, :]
            b_last = b[c - 1:c, :]
        else:
            ref = b[c // 2:c // 2 + 1, :]
            b_last = b[0:1, :]
        qd = hq * jnp.exp(b - ref)
        kd = kk_ref[:, wide] * jnp.exp(ref - b)
        p_ref[slot, d, 0, :, cols] = qd.astype(BF16)
        p_ref[slot, d, 1, :, cols] = kd.astype(BF16)
        p_ref[slot, d, 2, :, cols] = (qd * jnp.exp(ref)).astype(BF16)
        p_ref[slot, d, 3, :, cols] = (kd * jnp.exp(b_last - ref)).astype(BF16)
        dc_ref[slot, d, 0:1, cols] = jnp.exp(b_last)

    units = [(d, hd) for d in range(2) for hd in range(HGRN_HEADS)]
    strips = [(d, blk) for blk in range(n_strips) for d in range(2)]

    def step(slot, t, prep_next):
        if prep_next:
            for d, blk in strips:
                stage1(t + 1, d, blk)
        for d, hd in units:
            cols = slice(hd * HGRN_DK, (hd + 1) * HGRN_DK)
            a = lax.dot_general(p_ref[slot, d, 0, :, cols], p_ref[slot, d, 1, :, cols], nt,
                                preferred_element_type=F32)
            a_ref[d * HGRN_HEADS + hd] = jnp.where(keep[d], a, 0.0).astype(BF16)
        if prep_next:
            cumsum(0)
            cumsum(1)
        for d, hd in units:
            cols = slice(hd * HGRN_DK, (hd + 1) * HGRN_DK)
            vh = ins[d][2][chunk_rows(d, t), cols]
            u_ref[d * HGRN_HEADS + hd] = lax.dot_general(vh, p_ref[slot, d, 3, :, cols], tn,
                                                         preferred_element_type=F32)
        for d, hd in units:
            cols = slice(hd * HGRN_DK, (hd + 1) * HGRN_DK)
            idx = d * HGRN_HEADS + hd
            rows = chunk_rows(d, t)
            o = lax.dot_general(p_ref[slot, d, 2, :, cols], sbf_ref[idx], nt, preferred_element_type=F32)
            o = o + jnp.dot(a_ref[idx], ins[d][2][rows, cols], preferred_element_type=F32)
            ins[d][3][rows, cols] = o.astype(ins[d][3].dtype)
        if prep_next:
            for d, blk in strips:
                stage3(1 - slot, t + 1, d, blk)
        for d, hd in units:
            cols = slice(hd * HGRN_DK, (hd + 1) * HGRN_DK)
            idx = d * HGRN_HEADS + hd
            st = s_ref[idx] * dc_ref[slot, d, 0:1, cols] + u_ref[idx]
            s_ref[idx] = st
            sbf_ref[idx] = st.astype(BF16)

    for d, blk in strips:
        stage1(0, d, blk)
    cumsum(0)
    cumsum(1)
    for d, blk in strips:
        stage3(0, 0, d, blk)
    for t in range(n_chunks):
        step(t % 2, t, t + 1 < n_chunks)


def _hgrn(hq, xf_f, xf_b, v, lbl, batch, seq_len):
    tm = TOKEN_TILE
    nblk = seq_len // tm
    n = v.shape[0]
    w = HGRN_WIDTH
    fwd = pl.BlockSpec((tm, w), lambda b, i: (b * nblk + i, 0))
    bwd = pl.BlockSpec((tm, w), lambda b, i: (b * nblk + nblk - 1 - i, 0))
    out = jax.ShapeDtypeStruct((n, w), BF16)
    nu = 2 * HGRN_HEADS
    return pl.pallas_call(
        _hgrn_kernel,
        grid=(batch, nblk),
        in_specs=[fwd, fwd, fwd, bwd, bwd, bwd, _const_spec(lbl.shape)],
        out_specs=[fwd, bwd],
        out_shape=[out, out],
        scratch_shapes=[pltpu.VMEM((nu, HGRN_DK, HGRN_DK), F32),
                        pltpu.VMEM((nu, HGRN_DK, HGRN_DK), BF16),
                        pltpu.VMEM((2, 2, 4, HGRN_CHUNK, w), BF16),
                        pltpu.VMEM((2, 2, SUBLANES, w), F32),
                        pltpu.VMEM((nu, HGRN_CHUNK, HGRN_CHUNK), BF16),
                        pltpu.VMEM((nu, HGRN_DK, HGRN_DK), F32),
                        pltpu.VMEM((3, HGRN_CHUNK, 2 * w), BF16),
                        pltpu.VMEM((HGRN_CHUNK, 2 * w), F32),
                        pltpu.VMEM((HGRN_CHUNK, 2 * w), F32)],
        compiler_params=pltpu.CompilerParams(
            dimension_semantics=("arbitrary", "arbitrary"), vmem_limit_bytes=VMEM_LIMIT_BYTES),
        name="hgrn",
    )(hq, xf_f, v, hq, xf_b, v, lbl)


def _mix_kernel(x_ref, a_ref, of_ref, ob_ref, nw_ref, wg_ref,
                gw_ref, wa_ref, wh_ref, wm_ref, x1_ref, h_ref):
    h_ref[...] = _rms(x_ref[...], nw_ref[...]).astype(BF16)

    def gate(lo, hi):
        return jnp.dot(h_ref[...], wg_ref[:, lo - _OFF_HG:hi - _OFF_HG], preferred_element_type=F32)

    o = of_ref[...].astype(F32) + ob_ref[...].astype(F32)
    hg = gate(_OFF_HG, _OFF_GA)
    gw = gw_ref[...]
    parts = []
    for hd in range(HGRN_HEADS):
        cols = slice(hd * HGRN_DK, (hd + 1) * HGRN_DK)
        oh = o[:, cols]
        ms = jnp.mean(oh * oh, axis=-1, keepdims=True)
        parts.append(oh * lax.rsqrt(ms + NORM_EPS))
    on = jnp.concatenate(parts, axis=1)
    rr = (on * gw * (hg * jax.nn.sigmoid(hg))).astype(BF16)
    wa = jnp.concatenate([wa_ref[h * HEAD_DIM:(h + 1) * HEAD_DIM, :] for h in _HEAD_PAIR_ORDER], axis=0)
    pa = jnp.dot(a_ref[...], wa, preferred_element_type=F32)
    ph = jnp.dot(rr, wh_ref[...], preferred_element_type=F32)
    mixed = (jax.nn.sigmoid(gate(_OFF_GA, _OFF_GH)) * pa
             + jax.nn.sigmoid(gate(_OFF_GH, D_IN)) * ph).astype(BF16)
    x1_ref[...] = x_ref[...] + jnp.dot(mixed, wm_ref[...], preferred_element_type=F32)


def _mix(x2, a, of, ob, nw, wg, gw, wa, wh, wm):
    n = x2.shape[0]
    tm = TOKEN_TILE
    row = lambda i: (i, 0)
    tok = lambda w: pl.BlockSpec((tm, w), row)
    return pl.pallas_call(
        _mix_kernel,
        grid=(n // tm,),
        in_specs=[tok(D_MODEL), tok(ATTN_WIDTH), tok(HGRN_WIDTH), tok(HGRN_WIDTH),
                  _const_spec((1, D_MODEL)), _const_spec(wg.shape),
                  _const_spec((1, HGRN_WIDTH)), _const_spec((ATTN_WIDTH, D_MODEL)),
                  _const_spec((HGRN_WIDTH, D_MODEL)), _const_spec((D_MODEL, D_MODEL))],
        out_specs=tok(D_MODEL),
        out_shape=jax.ShapeDtypeStruct((n, D_MODEL), F32),
        scratch_shapes=[pltpu.VMEM((tm, D_MODEL), BF16)],
        compiler_params=pltpu.CompilerParams(
            dimension_semantics=("arbitrary",), vmem_limit_bytes=VMEM_LIMIT_BYTES),
        name="mix",
    )(x2, a, of, ob, nw, wg, gw, wa, wh, wm)


def _ffn_kernel(xp_ref, xc_ref, xn_ref, nw_ref, wu_ref, cw_ref, cb_ref, wd_ref, fw_ref,
                y_ref, hb_ref, hbf_ref, u_ref, act_ref, *, tiles_per_seq):
    i = pl.program_id(0)
    tm = xc_ref.shape[0]
    hl = SUBLANES
    nw = nw_ref[...]
    pos = i % tiles_per_seq
    keep_prev = jnp.where(pos != 0, 1.0, 0.0)
    keep_next = jnp.where(pos != tiles_per_seq - 1, 1.0, 0.0)
    hb_ref[0:hl, :] = _rms(xp_ref[...], nw) * keep_prev
    hb_ref[hl:hl + tm, :] = _rms(xc_ref[...], nw)
    hb_ref[hl + tm:, :] = _rms(xn_ref[...], nw) * keep_next
    hbf_ref[...] = hb_ref[...].astype(BF16)

    ck = FFN_COL_CHUNK
    nck = D_FF // ck
    nlb = ck // LANES

    def up(j):
        lo, hi = j * ck, (j + 1) * ck
        uv = jnp.dot(hbf_ref[...], wu_ref[:, lo:hi], preferred_element_type=F32)
        ug = jnp.dot(hbf_ref[...], wu_ref[:, D_FF + lo:D_FF + hi], preferred_element_type=F32)
        for k in range(nlb):
            u_ref[j, k] = uv[:, k * LANES:(k + 1) * LANES]
            u_ref[j, nlb + k] = ug[:, k * LANES:(k + 1) * LANES]

    def conv(j, k, col):
        cs = slice(col, col + LANES)
        return (cw_ref[0:1, cs] * u_ref[j, k, hl - 1:hl - 1 + tm, :]
                + cw_ref[1:2, cs] * u_ref[j, k, hl:hl + tm, :]
                + cw_ref[2:3, cs] * u_ref[j, k, hl + 1:hl + 1 + tm, :]
                + cb_ref[:, cs])

    def act(j):
        for k in range(nlb):
            col = j * ck + k * LANES
            val = conv(j, k, col)
            gate = conv(j, nlb + k, D_FF + col)
            act_ref[:, col:col + LANES] = (gate * jax.nn.sigmoid(gate) * val).astype(BF16)

    up(0)
    for j in range(nck):
        if j + 1 < nck:
            up(j + 1)
        act(j)
    y = xc_ref[...] + jnp.dot(act_ref[...], wd_ref[...], preferred_element_type=F32)
    y_ref[...] = _rms(y, fw_ref[...])


def _ffn(x1, nw, wu, cw, cb, wd, fw, seq_len):
    n = x1.shape[0]
    tm = TOKEN_TILE
    hl = SUBLANES
    per = tm // hl
    nhalo = n // hl
    row = lambda i: (i, 0)
    prev = lambda i: (jnp.maximum(i * per - 1, 0), 0)
    nxt = lambda i: (jnp.minimum((i + 1) * per, nhalo - 1), 0)
    kern = functools.partial(_ffn_kernel, tiles_per_seq=seq_len // tm)
    return pl.pallas_call(
        kern,
        grid=(n // tm,),
        in_specs=[pl.BlockSpec((hl, D_MODEL), prev), pl.BlockSpec((tm, D_MODEL), row),
                  pl.BlockSpec((hl, D_MODEL), nxt),
                  _const_spec((1, D_MODEL)), _const_spec((D_MODEL, 2 * D_FF)),
                  _const_spec((3, 2 * D_FF)), _const_spec((1, 2 * D_FF)),
                  _const_spec((D_FF, D_MODEL)), _const_spec((1, D_MODEL))],
        out_specs=pl.BlockSpec((tm, D_MODEL), row),
        out_shape=jax.ShapeDtypeStruct((n, D_MODEL), F32),
        scratch_shapes=[pltpu.VMEM((tm + 2 * hl, D_MODEL), F32),
                        pltpu.VMEM((tm + 2 * hl, D_MODEL), BF16),
                        pltpu.VMEM((D_FF // FFN_COL_CHUNK, 2 * FFN_COL_CHUNK // LANES, tm + 2 * hl, LANES), F32),
                        pltpu.VMEM((tm, D_FF), BF16)],
        compiler_params=pltpu.CompilerParams(
            dimension_semantics=("arbitrary",), vmem_limit_bytes=VMEM_LIMIT_BYTES),
        name="ffn",
    )(x1, x1, x1, nw, wu, cw, cb, wd, fw)


def _rope_tables(seq_len):
    half = HEAD_DIM // 2
    inv_freq = ROPE_THETA ** (-jnp.arange(half, dtype=F32) / half)
    ang = jnp.arange(seq_len, dtype=F32)[:, None] * inv_freq[None, :]
    cos = jnp.cos(ang)
    sin = jnp.sin(ang)
    reps = LANES // HEAD_DIM
    cos_t = jnp.tile(jnp.concatenate([cos, cos], axis=1), (1, reps))
    sin_t = jnp.tile(jnp.concatenate([-sin, sin], axis=1), (1, reps))
    return cos_t, sin_t


def _trunk(x, p):
    batch, seq_len, _ = x.shape
    n = batch * seq_len
    x2 = x.reshape(n, D_MODEL)
    cos, sin = p["rope"][0][:seq_len], p["rope"][1][:seq_len]
    q, k, v, hq, xf_f, xf_b, hi = _inproj(x2, p["norm_mix_w"], p["w_in"], cos, sin, seq_len)
    a = _attn(p["sink"], q, k, v, batch, seq_len)
    of, ob = _hgrn(hq, xf_f, xf_b, hi, p["lbl"], batch, seq_len)
    x1 = _mix(x2, a, of, ob, p["norm_mix_w"], p["w_gate"], p["hgrn_norm_w"], p["w_attn_out"], p["w_hgrn_out"],
              p["w_mix_out"])
    y = _ffn(x1, p["norm_ffn_w"], p["w_up"], p["conv_w"], p["conv_b"], p["w_down"], p["norm_final_w"], seq_len)
    return y.reshape(batch, seq_len, D_MODEL)


def kernel(x_prompt, x_sample, norm_mix_w, w_in, attn_sink, hgrn_lb_logits, hgrn_norm_w, w_attn_out, w_hgrn_out, w_mix_out, norm_ffn_w, w_up, conv_w, conv_b, w_down, norm_final_w):
    p = {
        "rope": _rope_tables(max(x_prompt.shape[1], x_sample.shape[1])),
        "norm_mix_w": norm_mix_w[0].reshape(1, D_MODEL),
        "w_in": w_in[0, :, :_OFF_HG].astype(BF16),
        "w_gate": w_in[0, :, _OFF_HG:].astype(BF16),
        "sink": attn_sink[0].astype(F32),
        "lbl": hgrn_lb_logits.astype(F32).reshape(hgrn_lb_logits.shape[0], 2 * HGRN_WIDTH),
        "hgrn_norm_w": hgrn_norm_w[0].astype(F32).reshape(1, HGRN_WIDTH),
        "w_attn_out": w_attn_out[0].astype(BF16),
        "w_hgrn_out": w_hgrn_out[0].astype(BF16),
        "w_mix_out": w_mix_out[0].astype(BF16),
        "norm_ffn_w": norm_ffn_w[0].reshape(1, D_MODEL),
        "w_up": w_up[0].astype(BF16),
        "conv_w": conv_w[0],
        "conv_b": conv_b[0].reshape(1, 2 * D_FF),
        "w_down": w_down[0].astype(BF16),
        "norm_final_w": norm_final_w.reshape(1, D_MODEL),
    }
    return (_trunk(x_prompt, p), _trunk(x_sample, p))
```

```python
import functools

import jax
import jax.numpy as jnp
from jax import lax
from jax.experimental import pallas as pl
from jax.experimental.pallas import tpu as pltpu

D_MODEL = 1024
HEAD_DIM = 64
N_Q_HEADS = 8
N_KV_HEADS = 2
ATTN_WIDTH = N_Q_HEADS * HEAD_DIM
KV_WIDTH = N_KV_HEADS * HEAD_DIM
WINDOW = 128
ROPE_THETA = 10000.0
HGRN_HEADS = 4
HGRN_DK = 128
HGRN_WIDTH = HGRN_HEADS * HGRN_DK
HGRN_CHUNK = 64
D_FF = 2816
NORM_EPS = 1e-6

_OFF_Q = 0
_OFF_K = _OFF_Q + ATTN_WIDTH
_OFF_V = _OFF_K + KV_WIDTH
_OFF_HQ = _OFF_V + KV_WIDTH
_OFF_HFF = _OFF_HQ + HGRN_WIDTH
_OFF_HFB = _OFF_HFF + HGRN_WIDTH
_OFF_HI = _OFF_HFB + HGRN_WIDTH
_OFF_HG = _OFF_HI + HGRN_WIDTH
_OFF_GA = _OFF_HG + HGRN_WIDTH
_OFF_GH = _OFF_GA + D_MODEL
D_IN = _OFF_GH + D_MODEL

LANES = 128
SUBLANES = 8
VMEM_LIMIT_BYTES = 56 * 1024 * 1024

TOKEN_TILE = 512
INPROJ_TILE = 1024
ATTN_TILE = 512
FFN_COL_CHUNK = 256

BF16 = jnp.bfloat16
F32 = jnp.float32

_HEAD_PAIR_ORDER = tuple(h for j in range(N_Q_HEADS // 2) for h in (j, j + N_Q_HEADS // 2))


def _const_spec(shape):
    nd = len(shape)
    return pl.BlockSpec(shape, lambda *_: (0,) * nd, pipeline_mode=pl.Buffered(1))


def _rms(x, w):
    ms = jnp.mean(x * x, axis=-1, keepdims=True)
    return x * lax.rsqrt(ms + NORM_EPS) * w


def _rotate_half_pairs(x):
    lane = lax.broadcasted_iota(jnp.int32, x.shape, 1)
    first_half = (lane % HEAD_DIM) < (HEAD_DIM // 2)
    fwd = pltpu.roll(x, LANES - HEAD_DIM // 2, 1)
    bwd = pltpu.roll(x, HEAD_DIM // 2, 1)
    return jnp.where(first_half, fwd, bwd)


def _inproj_kernel(x_ref, nw_ref, w_ref, cos_ref, sin_ref,
                   q_ref, k_ref, v_ref, hq_ref, xff_ref, xfb_ref, hi_ref, h_ref):
    h_ref[...] = _rms(x_ref[...], nw_ref[...]).astype(BF16)

    def proj(lo, hi):
        return jnp.dot(h_ref[...], w_ref[:, lo:hi], preferred_element_type=F32)

    def rope(z):
        return z * cos_ref[...] + _rotate_half_pairs(z) * sin_ref[...]

    zq = proj(_OFF_Q, _OFF_K)
    n_groups = ATTN_WIDTH // LANES
    grp = [rope(zq[:, g * LANES:(g + 1) * LANES]) * (HEAD_DIM ** -0.5) for g in range(n_groups)]
    lo_half = lax.broadcasted_iota(jnp.int32, grp[0].shape, 1) < HEAD_DIM
    swap = lambda z: pltpu.roll(z, HEAD_DIM, 1)
    for g in range(n_groups // 2):
        a, b = grp[g], grp[g + n_groups // 2]
        q_ref[:, 2 * g * LANES:(2 * g + 1) * LANES] = jnp.where(lo_half, a, swap(b)).astype(BF16)
        q_ref[:, (2 * g + 1) * LANES:(2 * g + 2) * LANES] = jnp.where(lo_half, swap(a), b).astype(BF16)
    k_ref[...] = rope(proj(_OFF_K, _OFF_V)).astype(BF16)
    v_ref[...] = proj(_OFF_V, _OFF_HQ).astype(BF16)
    hq_ref[...] = proj(_OFF_HQ, _OFF_HFF)
    xff_ref[...] = proj(_OFF_HFF, _OFF_HFB)
    xfb_ref[...] = proj(_OFF_HFB, _OFF_HI)
    hi_ref[...] = proj(_OFF_HI, _OFF_HG).astype(BF16)


def _inproj(x2, nw, w_in, cos, sin, seq_len):
    n = x2.shape[0]
    tm = INPROJ_TILE
    tiles_per_seq = seq_len // tm
    row = lambda i: (i, 0)
    pos = lambda i: (i % tiles_per_seq, 0)
    tok = lambda w: pl.BlockSpec((tm, w), row)
    widths = (ATTN_WIDTH, KV_WIDTH, KV_WIDTH, HGRN_WIDTH, HGRN_WIDTH, HGRN_WIDTH, HGRN_WIDTH)
    dtypes = (BF16, BF16, BF16, F32, F32, F32, BF16)
    shapes = [jax.ShapeDtypeStruct((n, w), dt) for w, dt in zip(widths, dtypes)]
    return pl.pallas_call(
        _inproj_kernel,
        grid=(n // tm,),
        in_specs=[tok(D_MODEL), _const_spec((1, D_MODEL)), _const_spec(w_in.shape),
                  pl.BlockSpec((tm, LANES), pos), pl.BlockSpec((tm, LANES), pos)],
        out_specs=[tok(w) for w in widths],
        out_shape=shapes,
        scratch_shapes=[pltpu.VMEM((tm, D_MODEL), BF16)],
        compiler_params=pltpu.CompilerParams(
            dimension_semantics=("arbitrary",), vmem_limit_bytes=VMEM_LIMIT_BYTES),
        name="inproj",
    )(x2, nw, w_in, cos, sin)


def _mixattn_kernel(sink_ref, x_ref, q_ref, kp_ref, kc_ref, kn_ref, vp_ref, vc_ref, vn_ref, of_ref, ob_ref,
                    nw_ref, wg_ref, gw_ref, wa_ref, wh_ref, wm_ref, x1_ref,
                    kw_ref, vw_ref, sc_ref, h_ref, as_ref, g_ref):
    h_ref[...] = _rms(x_ref[...], nw_ref[...]).astype(BF16)
    i = pl.program_id(1)
    nblk = pl.num_programs(1)
    tq = q_ref.shape[0]
    w = WINDOW
    kw_ref[0:w, :] = kp_ref[...]
    kw_ref[w:w + tq, :] = kc_ref[...]
    kw_ref[w + tq:, :] = kn_ref[...]
    vw_ref[0:w, :] = vp_ref[...]
    vw_ref[w:w + tq, :] = vc_ref[...]
    vw_ref[w + tq:, :] = vn_ref[...]

    lane = lax.broadcasted_iota(jnp.int32, (w, LANES), 1)
    lo_half = lane < HEAD_DIM
    lane_v = lax.broadcasted_iota(jnp.int32, (3 * w, LANES), 1)
    lo_half_v = lane_v < HEAD_DIM
    r = lax.broadcasted_iota(jnp.int32, (w, w), 0)
    cc = lax.broadcasted_iota(jnp.int32, (w, w), 1)
    band_prev = cc >= r
    band_next = cc <= r
    n_pairs = N_Q_HEADS // 2
    n_sub = tq // w

    def attn_sub(s):
        ok_prev = band_prev if s > 0 else jnp.logical_and(band_prev, i != 0)
        ok_next = band_next if s < n_sub - 1 else jnp.logical_and(band_next, i != nblk - 1)
        kwin = kw_ref[s * w:s * w + 3 * w, :]
        vwin = vw_ref[s * w:s * w + 3 * w, :]
        zero_v = jnp.zeros_like(vwin)
        vblk = jnp.concatenate([jnp.where(lo_half_v, vwin, zero_v),
                                jnp.where(lo_half_v, zero_v, vwin)], axis=0)
        parts = []
        for j in range(n_pairs):
            qj = q_ref[s * w:(s + 1) * w, j * LANES:(j + 1) * LANES]
            zero_q = jnp.zeros_like(qj)
            parts.append(jnp.where(lo_half, qj, zero_q))
            parts.append(jnp.where(lo_half, zero_q, qj))
        lhs = jnp.concatenate(parts, axis=0)
        sc_ref[...] = lax.dot_general(lhs, kwin, (((1,), (1,)), ((), ())),
                                      preferred_element_type=F32)
        for j in range(n_pairs):
            pn = []
            inv = []
            for t in range(2):
                head = j + t * n_pairs
                rows = slice((2 * j + t) * w, (2 * j + t + 1) * w)
                s0 = jnp.where(ok_prev, sc_ref[rows, 0:w], -jnp.inf)
                s1 = sc_ref[rows, w:2 * w]
                s2 = jnp.where(ok_next, sc_ref[rows, 2 * w:3 * w], -jnp.inf)
                sink = sink_ref[head]
                m = jnp.max(jnp.maximum(jnp.maximum(s0, s1), s2), axis=-1, keepdims=True)
                m = jnp.maximum(m, sink)
                p0 = jnp.exp(s0 - m)
                p1 = jnp.exp(s1 - m)
                p2 = jnp.exp(s2 - m)
                denom = jnp.sum(p0 + p1 + p2, axis=-1, keepdims=True) + jnp.exp(sink - m)
                inv.append(1.0 / denom)
                pn += [p0.astype(BF16), p1.astype(BF16), p2.astype(BF16)]
            pcat = jnp.concatenate(pn, axis=1)
            o = jnp.dot(pcat, vblk, preferred_element_type=F32)
            o = o * jnp.where(lo_half, inv[0], inv[1])
            as_ref[s * w:(s + 1) * w, j * LANES:(j + 1) * LANES] = o.astype(BF16)

    piece = D_MODEL // 2
    n_piece = (D_IN - _OFF_HG) // piece
    for u in range(n_piece):
        g_ref[:, u * piece:(u + 1) * piece] = jnp.dot(h_ref[...], wg_ref[:, u * piece:(u + 1) * piece],
                                                      preferred_element_type=F32)
        if u < n_sub:
            attn_sub(u)

    def gate(lo, hi):
        return g_ref[:, lo - _OFF_HG:hi - _OFF_HG]

    o = of_ref[...].astype(F32) + ob_ref[...].astype(F32)
    hg = gate(_OFF_HG, _OFF_GA)
    gw = gw_ref[...]
    parts = []
    for hd in range(HGRN_HEADS):
        cols = slice(hd * HGRN_DK, (hd + 1) * HGRN_DK)
        oh = o[:, cols]
        ms = jnp.mean(oh * oh, axis=-1, keepdims=True)
        parts.append(oh * lax.rsqrt(ms + NORM_EPS))
    on = jnp.concatenate(parts, axis=1)
    rr = (on * gw * (hg * jax.nn.sigmoid(hg))).astype(BF16)
    wa = jnp.concatenate([wa_ref[h * HEAD_DIM:(h + 1) * HEAD_DIM, :] for h in _HEAD_PAIR_ORDER], axis=0)
    pa = jnp.dot(as_ref[...], wa, preferred_element_type=F32)
    ph = jnp.dot(rr, wh_ref[...], preferred_element_type=F32)
    mixed = (jax.nn.sigmoid(gate(_OFF_GA, _OFF_GH)) * pa
             + jax.nn.sigmoid(gate(_OFF_GH, D_IN)) * ph).astype(BF16)
    x1_ref[...] = x_ref[...] + jnp.dot(mixed, wm_ref[...], preferred_element_type=F32)


def _mixattn(sink, x2, q, k, v, of, ob, nw, wg, gw, wa, wh, wm, batch, seq_len):
    tq = ATTN_TILE
    w = WINDOW
    nblk = seq_len // tq
    sub = tq // w
    n = x2.shape[0]
    nrow_blocks = batch * seq_len // w
    cur = lambda b, i, *_: (b * nblk + i, 0)
    prev = lambda b, i, *_: (jnp.maximum((b * nblk + i) * sub - 1, 0), 0)
    nxt = lambda b, i, *_: (jnp.minimum((b * nblk + i + 1) * sub, nrow_blocks - 1), 0)
    kv_cur = pl.BlockSpec((tq, KV_WIDTH), cur)
    kv_prev = pl.BlockSpec((w, KV_WIDTH), prev)
    kv_next = pl.BlockSpec((w, KV_WIDTH), nxt)
    tok = lambda wd: pl.BlockSpec((tq, wd), cur)
    return pl.pallas_call(
        _mixattn_kernel,
        grid_spec=pltpu.PrefetchScalarGridSpec(
            num_scalar_prefetch=1,
            grid=(batch, nblk),
            in_specs=[tok(D_MODEL), tok(ATTN_WIDTH), kv_prev, kv_cur, kv_next, kv_prev, kv_cur, kv_next,
                      tok(HGRN_WIDTH), tok(HGRN_WIDTH),
                      _const_spec((1, D_MODEL)), _const_spec(wg.shape),
                      _const_spec((1, HGRN_WIDTH)), _const_spec((ATTN_WIDTH, D_MODEL)),
                      _const_spec((HGRN_WIDTH, D_MODEL)), _const_spec((D_MODEL, D_MODEL))],
            out_specs=tok(D_MODEL),
            scratch_shapes=[pltpu.VMEM((tq + 2 * w, KV_WIDTH), BF16),
                            pltpu.VMEM((tq + 2 * w, KV_WIDTH), BF16),
                            pltpu.VMEM((N_Q_HEADS * w, 3 * w), F32),
                            pltpu.VMEM((tq, D_MODEL), BF16),
                            pltpu.VMEM((tq, ATTN_WIDTH), BF16),
                            pltpu.VMEM((tq, D_IN - _OFF_HG), F32)],
        ),
        out_shape=jax.ShapeDtypeStruct((n, D_MODEL), F32),
        compiler_params=pltpu.CompilerParams(
            dimension_semantics=("arbitrary", "arbitrary"), vmem_limit_bytes=VMEM_LIMIT_BYTES),
        name="mixattn",
    )(sink, x2, q, k, k, k, v, v, v, of, ob, nw, wg, gw, wa, wh, wm)


def _hgrn_kernel(hq_f_ref, xf_f_ref, v_f_ref, hq_b_ref, xf_b_ref, v_b_ref, lbl_ref,
                 of_ref, ob_ref,
                 s_ref, sbf_ref, p_ref, dc_ref, a_ref, u_ref, gp_ref, kk_ref, b_ref):
    @pl.when(pl.program_id(1) == 0)
    def _():
        s_ref[...] = jnp.zeros_like(s_ref)
        sbf_ref[...] = jnp.zeros_like(sbf_ref)

    c = HGRN_CHUNK
    n_chunks = of_ref.shape[0] // c
    r = lax.broadcasted_iota(jnp.int32, (c, c), 0)
    cc = lax.broadcasted_iota(jnp.int32, (c, c), 1)
    keep = (r >= cc, r <= cc)
    tri = (keep[0].astype(BF16), keep[1].astype(BF16))
    nt = (((1,), (1,)), ((), ()))
    tn = (((0,), (0,)), ((), ()))
    ins = ((hq_f_ref, xf_f_ref, v_f_ref, of_ref), (hq_b_ref, xf_b_ref, v_b_ref, ob_ref))
    n_strips = HGRN_WIDTH // LANES

    lbl = lbl_ref[...]
    e = jnp.exp(lbl - jnp.max(lbl, axis=0, keepdims=True))
    lb = e[0:1, :] / jnp.sum(e, axis=0, keepdims=True)

    def chunk_rows(d, t):
        ci = t if d == 0 else n_chunks - 1 - t
        return slice(ci * c, (ci + 1) * c)

    def stage1(t, d, blk):
        rows = chunk_rows(d, t)
        cols = slice(blk * LANES, (blk + 1) * LANES)
        wide = slice(d * HGRN_WIDTH + blk * LANES, d * HGRN_WIDTH + (blk + 1) * LANES)
        xf = ins[d][1][rows, cols]
        lbd = lb[:, wide]
        f = lbd + (1.0 - lbd) * jax.nn.sigmoid(xf)
        kk_ref[:, wide] = 1.0 - f
        g = jnp.log(f)
        g1 = g.astype(BF16)
        r1 = g - g1.astype(F32)
        g2 = r1.astype(BF16)
        gp_ref[0, :, wide] = g1
        gp_ref[1, :, wide] = g2
        gp_ref[2, :, wide] = (r1 - g2.astype(F32)).astype(BF16)

    def cumsum(d):
        wide = slice(d * HGRN_WIDTH, (d + 1) * HGRN_WIDTH)
        b_ref[:, wide] = (jnp.dot(tri[d], gp_ref[0, :, wide], preferred_element_type=F32)
                          + jnp.dot(tri[d], gp_ref[1, :, wide], preferred_element_type=F32)
                          + jnp.dot(tri[d], gp_ref[2, :, wide], preferred_element_type=F32))

    def stage3(slot, t, d, blk):
        rows = chunk_rows(d, t)
        cols = slice(blk * LANES, (blk + 1) * LANES)
        wide = slice(d * HGRN_WIDTH + blk * LANES, d * HGRN_WIDTH + (blk + 1) * LANES)
        hq = ins[d][0][rows, cols]
        b = b_ref[:, wide]
        if d == 0:
            ref = b[c // 2 - 1:c // 2, :]
            b_last = b[c - 1:c, :]
        else:
            ref = b[c // 2:c // 2 + 1, :]
            b_last = b[0:1, :]
        qd = hq * jnp.exp(b - ref)
        kd = kk_ref[:, wide] * jnp.exp(ref - b)
        p_ref[slot, d, 0, :, cols] = qd.astype(BF16)
        p_ref[slot, d, 1, :, cols] = kd.astype(BF16)
        p_ref[slot, d, 2, :, cols] = (qd * jnp.exp(ref)).astype(BF16)
        p_ref[slot, d, 3, :, cols] = (kd * jnp.exp(b_last - ref)).astype(BF16)
        dc_ref[slot, d, 0:1, cols] = jnp.exp(b_last)

    units = [(d, hd) for d in range(2) for hd in range(HGRN_HEADS)]
    strips = [(d, blk) for blk in range(n_strips) for d in range(2)]

    def step(slot, t, prep_next):
        if prep_next:
            for d, blk in strips:
                stage1(t + 1, d, blk)
        for d, hd in units:
            cols = slice(hd * HGRN_DK, (hd + 1) * HGRN_DK)
            a = lax.dot_general(p_ref[slot, d, 0, :, cols], p_ref[slot, d, 1, :, cols], nt,
                                preferred_element_type=F32)
            a_ref[d * HGRN_HEADS + hd] = jnp.where(keep[d], a, 0.0).astype(BF16)
        if prep_next:
            cumsum(0)
            cumsum(1)
        for d, hd in units:
            cols = slice(hd * HGRN_DK, (hd + 1) * HGRN_DK)
            vh = ins[d][2][chunk_rows(d, t), cols]
            u_ref[d * HGRN_HEADS + hd] = lax.dot_general(vh, p_ref[slot, d, 3, :, cols], tn,
                                                         preferred_element_type=F32)
        for d, hd in units:
            cols = slice(hd * HGRN_DK, (hd + 1) * HGRN_DK)
            idx = d * HGRN_HEADS + hd
            rows = chunk_rows(d, t)
            o = lax.dot_general(p_ref[slot, d, 2, :, cols], sbf_ref[idx], nt, preferred_element_type=F32)
            o = o + jnp.dot(a_ref[idx], ins[d][2][rows, cols], preferred_element_type=F32)
            ins[d][3][rows, cols] = o.astype(ins[d][3].dtype)
        if prep_next:
            for d, blk in strips:
                stage3(1 - slot, t + 1, d, blk)
        for d, hd in units:
            cols = slice(hd * HGRN_DK, (hd + 1) * HGRN_DK)
            idx = d * HGRN_HEADS + hd
            st = s_ref[idx] * dc_ref[slot, d, 0:1, cols] + u_ref[idx]
            s_ref[idx] = st
            sbf_ref[idx] = st.astype(BF16)

    for d, blk in strips:
        stage1(0, d, blk)
    cumsum(0)
    cumsum(1)
    for d, blk in strips:
        stage3(0, 0, d, blk)
    for t in range(n_chunks):
        step(t % 2, t, t + 1 < n_chunks)


def _hgrn(hq, xf_f, xf_b, v, lbl, batch, seq_len):
    tm = TOKEN_TILE
    nblk = seq_len // tm
    n = v.shape[0]
    w = HGRN_WIDTH
    fwd = pl.BlockSpec((tm, w), lambda b, i: (b * nblk + i, 0))
    bwd = pl.BlockSpec((tm, w), lambda b, i: (b * nblk + nblk - 1 - i, 0))
    out = jax.ShapeDtypeStruct((n, w), BF16)
    nu = 2 * HGRN_HEADS
    return pl.pallas_call(
        _hgrn_kernel,
        grid=(batch, nblk),
        in_specs=[fwd, fwd, fwd, bwd, bwd, bwd, _const_spec(lbl.shape)],
        out_specs=[fwd, bwd],
        out_shape=[out, out],
        scratch_shapes=[pltpu.VMEM((nu, HGRN_DK, HGRN_DK), F32),
                        pltpu.VMEM((nu, HGRN_DK, HGRN_DK), BF16),
                        pltpu.VMEM((2, 2, 4, HGRN_CHUNK, w), BF16),
                        pltpu.VMEM((2, 2, SUBLANES, w), F32),
                        pltpu.VMEM((nu, HGRN_CHUNK, HGRN_CHUNK), BF16),
                        pltpu.VMEM((nu, HGRN_DK, HGRN_DK), F32),
                        pltpu.VMEM((3, HGRN_CHUNK, 2 * w), BF16),
                        pltpu.VMEM((HGRN_CHUNK, 2 * w), F32),
                        pltpu.VMEM((HGRN_CHUNK, 2 * w), F32)],
        compiler_params=pltpu.CompilerParams(
            dimension_semantics=("arbitrary", "arbitrary"), vmem_limit_bytes=VMEM_LIMIT_BYTES),
        name="hgrn",
    )(hq, xf_f, v, hq, xf_b, v, lbl)


def _ffn_kernel(xp_ref, xc_ref, xn_ref, nw_ref, wu_ref, cw_ref, cb_ref, wd_ref, fw_ref,
                y_ref, hb_ref, hbf_ref, u_ref, act_ref, *, tiles_per_seq):
    i = pl.program_id(0)
    tm = xc_ref.shape[0]
    hl = SUBLANES
    nw = nw_ref[...]
    pos = i % tiles_per_seq
    keep_prev = jnp.where(pos != 0, 1.0, 0.0)
    keep_next = jnp.where(pos != tiles_per_seq - 1, 1.0, 0.0)
    hb_ref[0:hl, :] = _rms(xp_ref[...], nw) * keep_prev
    hb_ref[hl:hl + tm, :] = _rms(xc_ref[...], nw)
    hb_ref[hl + tm:, :] = _rms(xn_ref[...], nw) * keep_next
    hbf_ref[...] = hb_ref[...].astype(BF16)

    ck = FFN_COL_CHUNK
    nck = D_FF // ck
    nlb = ck // LANES

    def up(j):
        lo, hi = j * ck, (j + 1) * ck
        uv = jnp.dot(hbf_ref[...], wu_ref[:, lo:hi], preferred_element_type=F32)
        ug = jnp.dot(hbf_ref[...], wu_ref[:, D_FF + lo:D_FF + hi], preferred_element_type=F32)
        for k in range(nlb):
            u_ref[j, k] = uv[:, k * LANES:(k + 1) * LANES]
            u_ref[j, nlb + k] = ug[:, k * LANES:(k + 1) * LANES]

    def conv(j, k, col):
        cs = slice(col, col + LANES)
        return (cw_ref[0:1, cs] * u_ref[j, k, hl - 1:hl - 1 + tm, :]
                + cw_ref[1:2, cs] * u_ref[j, k, hl:hl + tm, :]
                + cw_ref[2:3, cs] * u_ref[j, k, hl + 1:hl + 1 + tm, :]
                + cb_ref[:, cs])

    def act(j):
        for k in range(nlb):
            col = j * ck + k * LANES
            val = conv(j, k, col)
            gate = conv(j, nlb + k, D_FF + col)
            act_ref[:, col:col + LANES] = (gate * jax.nn.sigmoid(gate) * val).astype(BF16)

    up(0)
    for j in range(nck):
        if j + 1 < nck:
            up(j + 1)
        act(j)
    y = xc_ref[...] + jnp.dot(act_ref[...], wd_ref[...], preferred_element_type=F32)
    y_ref[...] = _rms(y, fw_ref[...])


def _ffn(x1, nw, wu, cw, cb, wd, fw, seq_len):
    n = x1.shape[0]
    tm = TOKEN_TILE
    hl = SUBLANES
    per = tm // hl
    nhalo = n // hl
    row = lambda i: (i, 0)
    prev = lambda i: (jnp.maximum(i * per - 1, 0), 0)
    nxt = lambda i: (jnp.minimum((i + 1) * per, nhalo - 1), 0)
    kern = functools.partial(_ffn_kernel, tiles_per_seq=seq_len // tm)
    return pl.pallas_call(
        kern,
        grid=(n // tm,),
        in_specs=[pl.BlockSpec((hl, D_MODEL), prev), pl.BlockSpec((tm, D_MODEL), row),
                  pl.BlockSpec((hl, D_MODEL), nxt),
                  _const_spec((1, D_MODEL)), _const_spec((D_MODEL, 2 * D_FF)),
                  _const_spec((3, 2 * D_FF)), _const_spec((1, 2 * D_FF)),
                  _const_spec((D_FF, D_MODEL)), _const_spec((1, D_MODEL))],
        out_specs=pl.BlockSpec((tm, D_MODEL), row),
        out_shape=jax.ShapeDtypeStruct((n, D_MODEL), F32),
        scratch_shapes=[pltpu.VMEM((tm + 2 * hl, D_MODEL), F32),
                        pltpu.VMEM((tm + 2 * hl, D_MODEL), BF16),
                        pltpu.VMEM((D_FF // FFN_COL_CHUNK, 2 * FFN_COL_CHUNK // LANES, tm + 2 * hl, LANES), F32),
                        pltpu.VMEM((tm, D_FF), BF16)],
        compiler_params=pltpu.CompilerParams(
            dimension_semantics=("arbitrary",), vmem_limit_bytes=VMEM_LIMIT_BYTES),
        name="ffn",
    )(x1, x1, x1, nw, wu, cw, cb, wd, fw)


def _rope_tables(seq_len):
    half = HEAD_DIM // 2
    inv_freq = ROPE_THETA ** (-jnp.arange(half, dtype=F32) / half)
    ang = jnp.arange(seq_len, dtype=F32)[:, None] * inv_freq[None, :]
    cos = jnp.cos(ang)
    sin = jnp.sin(ang)
    reps = LANES // HEAD_DIM
    cos_t = jnp.tile(jnp.concatenate([cos, cos], axis=1), (1, reps))
    sin_t = jnp.tile(jnp.concatenate([-sin, sin], axis=1), (1, reps))
    return cos_t, sin_t


def _trunk(x, p):
    batch, seq_len, _ = x.shape
    n = batch * seq_len
    x2 = x.reshape(n, D_MODEL)
    cos, sin = p["rope"][0][:seq_len], p["rope"][1][:seq_len]
    q, k, v, hq, xf_f, xf_b, hi = _inproj(x2, p["norm_mix_w"], p["w_in"], cos, sin, seq_len)
    of, ob = _hgrn(hq, xf_f, xf_b, hi, p["lbl"], batch, seq_len)
    x1 = _mixattn(p["sink"], x2, q, k, v, of, ob, p["norm_mix_w"], p["w_gate"], p["hgrn_norm_w"], p["w_attn_out"],
                  p["w_hgrn_out"], p["w_mix_out"], batch, seq_len)
    y = _ffn(x1, p["norm_ffn_w"], p["w_up"], p["conv_w"], p["conv_b"], p["w_down"], p["norm_final_w"], seq_len)
    return y.reshape(batch, seq_len, D_MODEL)


def kernel(x_prompt, x_sample, norm_mix_w, w_in, attn_sink, hgrn_lb_logits, hgrn_norm_w, w_attn_out, w_hgrn_out, w_mix_out, norm_ffn_w, w_up, conv_w, conv_b, w_down, norm_final_w):
    p = {
        "rope": _rope_tables(max(x_prompt.shape[1], x_sample.shape[1])),
        "norm_mix_w": norm_mix_w[0].reshape(1, D_MODEL),
        "w_in": w_in[0, :, :_OFF_HG].astype(BF16),
        "w_gate": w_in[0, :, _OFF_HG:].astype(BF16),
        "sink": attn_sink[0].astype(F32),
        "lbl": hgrn_lb_logits.astype(F32).reshape(hgrn_lb_logits.shape[0], 2 * HGRN_WIDTH),
        "hgrn_norm_w": hgrn_norm_w[0].astype(F32).reshape(1, HGRN_WIDTH),
        "w_attn_out": w_attn_out[0].astype(BF16),
        "w_hgrn_out": w_hgrn_out[0].astype(BF16),
        "w_mix_out": w_mix_out[0].astype(BF16),
        "norm_ffn_w": norm_ffn_w[0].reshape(1, D_MODEL),
        "w_up": w_up[0].astype(BF16),
        "conv_w": conv_w[0],
        "conv_b": conv_b[0].reshape(1, 2 * D_FF),
        "w_down": w_down[0].astype(BF16),
        "norm_final_w": norm_final_w.reshape(1, D_MODEL),
    }
    return (_trunk(x_prompt, p), _trunk(x_sample, p))
```

```python
import functools

import jax
import jax.numpy as jnp
from jax import lax
from jax.experimental import pallas as pl
from jax.experimental.pallas import tpu as pltpu

D_MODEL = 1024
HEAD_DIM = 64
N_Q_HEADS = 8
N_KV_HEADS = 2
ATTN_WIDTH = N_Q_HEADS * HEAD_DIM
KV_WIDTH = N_KV_HEADS * HEAD_DIM
WINDOW = 128
ROPE_THETA = 10000.0
HGRN_HEADS = 4
HGRN_DK = 128
HGRN_WIDTH = HGRN_HEADS * HGRN_DK
HGRN_CHUNK = 64
D_FF = 2816
NORM_EPS = 1e-6

_OFF_Q = 0
_OFF_K = _OFF_Q + ATTN_WIDTH
_OFF_V = _OFF_K + KV_WIDTH
_OFF_HQ = _OFF_V + KV_WIDTH
_OFF_HFF = _OFF_HQ + HGRN_WIDTH
_OFF_HFB = _OFF_HFF + HGRN_WIDTH
_OFF_HI = _OFF_HFB + HGRN_WIDTH
_OFF_HG = _OFF_HI + HGRN_WIDTH
_OFF_GA = _OFF_HG + HGRN_WIDTH
_OFF_GH = _OFF_GA + D_MODEL
D_IN = _OFF_GH + D_MODEL

LANES = 128
SUBLANES = 8
VMEM_LIMIT_BYTES = 56 * 1024 * 1024

TOKEN_TILE = 512
INPROJ_TILE = 1024
HGRN_TILE = 1024
ATTN_TILE = 512
FFN_COL_CHUNK = 256

BF16 = jnp.bfloat16
F32 = jnp.float32

_HEAD_PAIR_ORDER = tuple(h for j in range(N_Q_HEADS // 2) for h in (j, j + N_Q_HEADS // 2))


def _const_spec(shape):
    nd = len(shape)
    return pl.BlockSpec(shape, lambda *_: (0,) * nd, pipeline_mode=pl.Buffered(1))


def _rms(x, w):
    ms = jnp.mean(x * x, axis=-1, keepdims=True)
    return x * lax.rsqrt(ms + NORM_EPS) * w


def _rotate_half_pairs(x):
    lane = lax.broadcasted_iota(jnp.int32, x.shape, 1)
    first_half = (lane % HEAD_DIM) < (HEAD_DIM // 2)
    fwd = pltpu.roll(x, LANES - HEAD_DIM // 2, 1)
    bwd = pltpu.roll(x, HEAD_DIM // 2, 1)
    return jnp.where(first_half, fwd, bwd)


def _inproj_kernel(x_ref, nw_ref, w_ref, cos_ref, sin_ref,
                   q_ref, k_ref, v_ref, hq_ref, xff_ref, xfb_ref, hi_ref, h_ref):
    h_ref[...] = _rms(x_ref[...], nw_ref[...]).astype(BF16)

    def proj(lo, hi):
        return jnp.dot(h_ref[...], w_ref[:, lo:hi], preferred_element_type=F32)

    def rope(z):
        return z * cos_ref[...] + _rotate_half_pairs(z) * sin_ref[...]

    zq = proj(_OFF_Q, _OFF_K)
    n_groups = ATTN_WIDTH // LANES
    grp = [rope(zq[:, g * LANES:(g + 1) * LANES]) * (HEAD_DIM ** -0.5) for g in range(n_groups)]
    lo_half = lax.broadcasted_iota(jnp.int32, grp[0].shape, 1) < HEAD_DIM
    swap = lambda z: pltpu.roll(z, HEAD_DIM, 1)
    for g in range(n_groups // 2):
        a, b = grp[g], grp[g + n_groups // 2]
        q_ref[:, 2 * g * LANES:(2 * g + 1) * LANES] = jnp.where(lo_half, a, swap(b)).astype(BF16)
        q_ref[:, (2 * g + 1) * LANES:(2 * g + 2) * LANES] = jnp.where(lo_half, swap(a), b).astype(BF16)
    k_ref[...] = rope(proj(_OFF_K, _OFF_V)).astype(BF16)
    v_ref[...] = proj(_OFF_V, _OFF_HQ).astype(BF16)
    hq_ref[...] = proj(_OFF_HQ, _OFF_HFF).astype(BF16)
    xff_ref[...] = proj(_OFF_HFF, _OFF_HFB).astype(BF16)
    xfb_ref[...] = proj(_OFF_HFB, _OFF_HI).astype(BF16)
    hi_ref[...] = proj(_OFF_HI, _OFF_HG).astype(BF16)


def _inproj(x2, nw, w_in, cos, sin, seq_len):
    n = x2.shape[0]
    tm = INPROJ_TILE
    tiles_per_seq = seq_len // tm
    row = lambda i: (i, 0)
    pos = lambda i: (i % tiles_per_seq, 0)
    tok = lambda w: pl.BlockSpec((tm, w), row)
    widths = (ATTN_WIDTH, KV_WIDTH, KV_WIDTH, HGRN_WIDTH, HGRN_WIDTH, HGRN_WIDTH, HGRN_WIDTH)
    shapes = [jax.ShapeDtypeStruct((n, w), BF16) for w in widths]
    return pl.pallas_call(
        _inproj_kernel,
        grid=(n // tm,),
        in_specs=[tok(D_MODEL), _const_spec((1, D_MODEL)), _const_spec(w_in.shape),
                  pl.BlockSpec((tm, LANES), pos), pl.BlockSpec((tm, LANES), pos)],
        out_specs=[tok(w) for w in widths],
        out_shape=shapes,
        scratch_shapes=[pltpu.VMEM((tm, D_MODEL), BF16)],
        compiler_params=pltpu.CompilerParams(
            dimension_semantics=("arbitrary",), vmem_limit_bytes=VMEM_LIMIT_BYTES),
        name="inproj",
    )(x2, nw, w_in, cos, sin)


def _mixattn_kernel(sink_ref, x_ref, q_ref, kp_ref, kc_ref, kn_ref, vp_ref, vc_ref, vn_ref, of_ref, ob_ref,
                    nw_ref, wg_ref, gw_ref, wa_ref, wh_ref, wm_ref, x1_ref,
                    kw_ref, vw_ref, sc_ref, h_ref, as_ref, g_ref):
    h_ref[...] = _rms(x_ref[...], nw_ref[...]).astype(BF16)
    i = pl.program_id(1)
    nblk = pl.num_programs(1)
    tq = q_ref.shape[0]
    w = WINDOW
    kw_ref[0:w, :] = kp_ref[...]
    kw_ref[w:w + tq, :] = kc_ref[...]
    kw_ref[w + tq:, :] = kn_ref[...]
    vw_ref[0:w, :] = vp_ref[...]
    vw_ref[w:w + tq, :] = vc_ref[...]
    vw_ref[w + tq:, :] = vn_ref[...]

    lane = lax.broadcasted_iota(jnp.int32, (w, LANES), 1)
    lo_half = lane < HEAD_DIM
    lane_v = lax.broadcasted_iota(jnp.int32, (3 * w, LANES), 1)
    lo_half_v = lane_v < HEAD_DIM
    r = lax.broadcasted_iota(jnp.int32, (w, w), 0)
    cc = lax.broadcasted_iota(jnp.int32, (w, w), 1)
    band_prev = cc >= r
    band_next = cc <= r
    n_pairs = N_Q_HEADS // 2
    n_sub = tq // w

    def attn_sub(s):
        ok_prev = band_prev if s > 0 else jnp.logical_and(band_prev, i != 0)
        ok_next = band_next if s < n_sub - 1 else jnp.logical_and(band_next, i != nblk - 1)
        kwin = kw_ref[s * w:s * w + 3 * w, :]
        vwin = vw_ref[s * w:s * w + 3 * w, :]
        zero_v = jnp.zeros_like(vwin)
        vblk = jnp.concatenate([jnp.where(lo_half_v, vwin, zero_v),
                                jnp.where(lo_half_v, zero_v, vwin)], axis=0)
        parts = []
        for j in range(n_pairs):
            qj = q_ref[s * w:(s + 1) * w, j * LANES:(j + 1) * LANES]
            zero_q = jnp.zeros_like(qj)
            parts.append(jnp.where(lo_half, qj, zero_q))
            parts.append(jnp.where(lo_half, zero_q, qj))
        lhs = jnp.concatenate(parts, axis=0)
        sc_ref[...] = lax.dot_general(lhs, kwin, (((1,), (1,)), ((), ())),
                                      preferred_element_type=F32)
        for j in range(n_pairs):
            pn = []
            inv = []
            for t in range(2):
                head = j + t * n_pairs
                rows = slice((2 * j + t) * w, (2 * j + t + 1) * w)
                s0 = jnp.where(ok_prev, sc_ref[rows, 0:w], -jnp.inf)
                s1 = sc_ref[rows, w:2 * w]
                s2 = jnp.where(ok_next, sc_ref[rows, 2 * w:3 * w], -jnp.inf)
                sink = sink_ref[head]
                m = jnp.max(jnp.maximum(jnp.maximum(s0, s1), s2), axis=-1, keepdims=True)
                m = jnp.maximum(m, sink)
                p0 = jnp.exp(s0 - m)
                p1 = jnp.exp(s1 - m)
                p2 = jnp.exp(s2 - m)
                denom = jnp.sum(p0 + p1 + p2, axis=-1, keepdims=True) + jnp.exp(sink - m)
                inv.append(1.0 / denom)
                pn += [p0.astype(BF16), p1.astype(BF16), p2.astype(BF16)]
            pcat = jnp.concatenate(pn, axis=1)
            o = jnp.dot(pcat, vblk, preferred_element_type=F32)
            o = o * jnp.where(lo_half, inv[0], inv[1])
            as_ref[s * w:(s + 1) * w, j * LANES:(j + 1) * LANES] = o.astype(BF16)

    piece = D_MODEL // 2
    n_piece = (D_IN - _OFF_HG) // piece
    for u in range(max(n_piece, n_sub)):
        if u < n_piece:
            g_ref[:, u * piece:(u + 1) * piece] = jnp.dot(h_ref[...], wg_ref[:, u * piece:(u + 1) * piece],
                                                          preferred_element_type=F32)
        if u < n_sub:
            attn_sub(u)

    def gate(lo, hi):
        return g_ref[:, lo - _OFF_HG:hi - _OFF_HG]

    o = of_ref[...].astype(F32) + ob_ref[...].astype(F32)
    hg = gate(_OFF_HG, _OFF_GA)
    gw = gw_ref[...]
    parts = []
    for hd in range(HGRN_HEADS):
        cols = slice(hd * HGRN_DK, (hd + 1) * HGRN_DK)
        oh = o[:, cols]
        ms = jnp.mean(oh * oh, axis=-1, keepdims=True)
        parts.append(oh * lax.rsqrt(ms + NORM_EPS))
    on = jnp.concatenate(parts, axis=1)
    rr = (on * gw * (hg * jax.nn.sigmoid(hg))).astype(BF16)
    wa = jnp.concatenate([wa_ref[h * HEAD_DIM:(h + 1) * HEAD_DIM, :] for h in _HEAD_PAIR_ORDER], axis=0)
    pa = jnp.dot(as_ref[...], wa, preferred_element_type=F32)
    ph = jnp.dot(rr, wh_ref[...], preferred_element_type=F32)
    mixed = (jax.nn.sigmoid(gate(_OFF_GA, _OFF_GH)) * pa
             + jax.nn.sigmoid(gate(_OFF_GH, D_IN)) * ph).astype(BF16)
    x1_ref[...] = x_ref[...] + jnp.dot(mixed, wm_ref[...], preferred_element_type=F32)


def _mixattn(sink, x2, q, k, v, of, ob, nw, wg, gw, wa, wh, wm, batch, seq_len):
    tq = ATTN_TILE
    w = WINDOW
    nblk = seq_len // tq
    sub = tq // w
    n = x2.shape[0]
    nrow_blocks = batch * seq_len // w
    cur = lambda b, i, *_: (b * nblk + i, 0)
    prev = lambda b, i, *_: (jnp.maximum((b * nblk + i) * sub - 1, 0), 0)
    nxt = lambda b, i, *_: (jnp.minimum((b * nblk + i + 1) * sub, nrow_blocks - 1), 0)
    kv_cur = pl.BlockSpec((tq, KV_WIDTH), cur)
    kv_prev = pl.BlockSpec((w, KV_WIDTH), prev)
    kv_next = pl.BlockSpec((w, KV_WIDTH), nxt)
    tok = lambda wd: pl.BlockSpec((tq, wd), cur)
    return pl.pallas_call(
        _mixattn_kernel,
        grid_spec=pltpu.PrefetchScalarGridSpec(
            num_scalar_prefetch=1,
            grid=(batch, nblk),
            in_specs=[tok(D_MODEL), tok(ATTN_WIDTH), kv_prev, kv_cur, kv_next, kv_prev, kv_cur, kv_next,
                      tok(HGRN_WIDTH), tok(HGRN_WIDTH),
                      _const_spec((1, D_MODEL)), _const_spec(wg.shape),
                      _const_spec((1, HGRN_WIDTH)), _const_spec((ATTN_WIDTH, D_MODEL)),
                      _const_spec((HGRN_WIDTH, D_MODEL)), _const_spec((D_MODEL, D_MODEL))],
            out_specs=tok(D_MODEL),
            scratch_shapes=[pltpu.VMEM((tq + 2 * w, KV_WIDTH), BF16),
                            pltpu.VMEM((tq + 2 * w, KV_WIDTH), BF16),
                            pltpu.VMEM((N_Q_HEADS * w, 3 * w), F32),
                            pltpu.VMEM((tq, D_MODEL), BF16),
                            pltpu.VMEM((tq, ATTN_WIDTH), BF16),
                            pltpu.VMEM((tq, D_IN - _OFF_HG), F32)],
        ),
        out_shape=jax.ShapeDtypeStruct((n, D_MODEL), F32),
        compiler_params=pltpu.CompilerParams(
            dimension_semantics=("arbitrary", "arbitrary"), vmem_limit_bytes=VMEM_LIMIT_BYTES),
        name="mixattn",
    )(sink, x2, q, k, k, k, v, v, v, of, ob, nw, wg, gw, wa, wh, wm)


def _hgrn_kernel(hq_f_ref, xf_f_ref, v_f_ref, hq_b_ref, xf_b_ref, v_b_ref, lbl_ref,
                 of_ref, ob_ref,
                 s_ref, sbf_ref, p_ref, dc_ref, a_ref, u_ref, gp_ref, kk_ref, b_ref):
    @pl.when(pl.program_id(1) == 0)
    def _():
        s_ref[...] = jnp.zeros_like(s_ref)
        sbf_ref[...] = jnp.zeros_like(sbf_ref)

    c = HGRN_CHUNK
    n_chunks = of_ref.shape[0] // c
    r = lax.broadcasted_iota(jnp.int32, (c, c), 0)
    cc = lax.broadcasted_iota(jnp.int32, (c, c), 1)
    keep = (r >= cc, r <= cc)
    tri = (keep[0].astype(BF16), keep[1].astype(BF16))
    nt = (((1,), (1,)), ((), ()))
    tn = (((0,), (0,)), ((), ()))
    ins = ((hq_f_ref, xf_f_ref, v_f_ref, of_ref), (hq_b_ref, xf_b_ref, v_b_ref, ob_ref))
    n_strips = HGRN_WIDTH // LANES

    lbl = lbl_ref[...]
    e = jnp.exp(lbl - jnp.max(lbl, axis=0, keepdims=True))
    lb = e[0:1, :] / jnp.sum(e, axis=0, keepdims=True)

    def chunk_rows(d, t):
        ci = t if d == 0 else n_chunks - 1 - t
        return slice(ci * c, (ci + 1) * c)

    def stage1(t, d, blk):
        rows = chunk_rows(d, t)
        cols = slice(blk * LANES, (blk + 1) * LANES)
        wide = slice(d * HGRN_WIDTH + blk * LANES, d * HGRN_WIDTH + (blk + 1) * LANES)
        xf = ins[d][1][rows, cols].astype(F32)
        lbd = lb[:, wide]
        f = lbd + (1.0 - lbd) * jax.nn.sigmoid(xf)
        kk_ref[d * n_strips + blk] = 1.0 - f
        g = jnp.log(f)
        g1 = g.astype(BF16)
        r1 = g - g1.astype(F32)
        g2 = r1.astype(BF16)
        gp_ref[0, :, wide] = g1
        gp_ref[1, :, wide] = g2
        gp_ref[2, :, wide] = (r1 - g2.astype(F32)).astype(BF16)

    def cumsum(d):
        wide = slice(d * HGRN_WIDTH, (d + 1) * HGRN_WIDTH)
        b_ref[:, wide] = (jnp.dot(tri[d], gp_ref[0, :, wide], preferred_element_type=F32)
                          + jnp.dot(tri[d], gp_ref[1, :, wide], preferred_element_type=F32)
                          + jnp.dot(tri[d], gp_ref[2, :, wide], preferred_element_type=F32))

    def stage3(slot, t, d, blk):
        rows = chunk_rows(d, t)
        cols = slice(blk * LANES, (blk + 1) * LANES)
        wide = slice(d * HGRN_WIDTH + blk * LANES, d * HGRN_WIDTH + (blk + 1) * LANES)
        hq = ins[d][0][rows, cols].astype(F32)
        b = b_ref[:, wide]
        if d == 0:
            ref = b[c // 2 - 1:c // 2, :]
            b_last = b[c - 1:c, :]
        else:
            ref = b[c // 2:c // 2 + 1, :]
            b_last = b[0:1, :]
        qd = hq * jnp.exp(b - ref)
        kd = kk_ref[d * n_strips + blk] * jnp.exp(ref - b)
        p_ref[slot, d, 0, blk] = qd.astype(BF16)
        p_ref[slot, d, 1, blk] = kd.astype(BF16)
        p_ref[slot, d, 2, blk] = (qd * jnp.exp(ref)).astype(BF16)
        p_ref[slot, d, 3, blk] = (kd * jnp.exp(b_last - ref)).astype(BF16)
        dc_ref[slot, d, 0:1, cols] = jnp.exp(b_last)

    units = [(d, hd) for d in range(2) for hd in range(HGRN_HEADS)]
    strips = [(d, blk) for blk in range(n_strips) for d in range(2)]

    def step(slot, t, prep_next):
        if prep_next:
            for d, blk in strips:
                stage1(t + 1, d, blk)
        for d, hd in units:
            cols = slice(hd * HGRN_DK, (hd + 1) * HGRN_DK)
            a = lax.dot_general(p_ref[slot, d, 0, hd], p_ref[slot, d, 1, hd], nt,
                                preferred_element_type=F32)
            a_ref[d * HGRN_HEADS + hd] = jnp.where(keep[d], a, 0.0).astype(BF16)
        if prep_next:
            cumsum(0)
            cumsum(1)
        for d, hd in units:
            cols = slice(hd * HGRN_DK, (hd + 1) * HGRN_DK)
            vh = ins[d][2][chunk_rows(d, t), cols]
            u_ref[d * HGRN_HEADS + hd] = lax.dot_general(vh, p_ref[slot, d, 3, hd], tn,
                                                         preferred_element_type=F32)
        for d, hd in units:
            cols = slice(hd * HGRN_DK, (hd + 1) * HGRN_DK)
            idx = d * HGRN_HEADS + hd
            rows = chunk_rows(d, t)
            o = lax.dot_general(p_ref[slot, d, 2, hd], sbf_ref[idx], nt, preferred_element_type=F32)
            o = o + jnp.dot(a_ref[idx], ins[d][2][rows, cols], preferred_element_type=F32)
            ins[d][3][rows, cols] = o.astype(ins[d][3].dtype)
        if prep_next:
            for d, blk in strips:
                stage3(1 - slot, t + 1, d, blk)
        for d, hd in units:
            cols = slice(hd * HGRN_DK, (hd + 1) * HGRN_DK)
            idx = d * HGRN_HEADS + hd
            st = s_ref[idx] * dc_ref[slot, d, 0:1, cols] + u_ref[idx]
            s_ref[idx] = st
            sbf_ref[idx] = st.astype(BF16)

    for d, blk in strips:
        stage1(0, d, blk)
    cumsum(0)
    cumsum(1)
    for d, blk in strips:
        stage3(0, 0, d, blk)
    for t in range(n_chunks):
        step(t % 2, t, t + 1 < n_chunks)


def _hgrn(hq, xf_f, xf_b, v, lbl, batch, seq_len):
    tm = HGRN_TILE
    nblk = seq_len // tm
    n = v.shape[0]
    w = HGRN_WIDTH
    fwd = pl.BlockSpec((tm, w), lambda b, i: (b * nblk + i, 0))
    bwd = pl.BlockSpec((tm, w), lambda b, i: (b * nblk + nblk - 1 - i, 0))
    out = jax.ShapeDtypeStruct((n, w), BF16)
    nu = 2 * HGRN_HEADS
    return pl.pallas_call(
        _hgrn_kernel,
        grid=(batch, nblk),
        in_specs=[fwd, fwd, fwd, bwd, bwd, bwd, _const_spec(lbl.shape)],
        out_specs=[fwd, bwd],
        out_shape=[out, out],
        scratch_shapes=[pltpu.VMEM((nu, HGRN_DK, HGRN_DK), F32),
                        pltpu.VMEM((nu, HGRN_DK, HGRN_DK), BF16),
                        pltpu.VMEM((2, 2, 4, HGRN_HEADS, HGRN_CHUNK, HGRN_DK), BF16),
                        pltpu.VMEM((2, 2, SUBLANES, w), F32),
                        pltpu.VMEM((nu, HGRN_CHUNK, HGRN_CHUNK), BF16),
                        pltpu.VMEM((nu, HGRN_DK, HGRN_DK), F32),
                        pltpu.VMEM((3, HGRN_CHUNK, 2 * w), BF16),
                        pltpu.VMEM((nu, HGRN_CHUNK, HGRN_DK), F32),
                        pltpu.VMEM((HGRN_CHUNK, 2 * w), F32)],
        compiler_params=pltpu.CompilerParams(
            dimension_semantics=("arbitrary", "arbitrary"), vmem_limit_bytes=VMEM_LIMIT_BYTES),
        name="hgrn",
    )(hq, xf_f, v, hq, xf_b, v, lbl)


def _ffn_kernel(xp_ref, xc_ref, xn_ref, nw_ref, wu_ref, cw_ref, cb_ref, wd_ref, fw_ref,
                y_ref, hb_ref, hbf_ref, u_ref, act_ref, *, tiles_per_seq):
    i = pl.program_id(0)
    tm = xc_ref.shape[0]
    hl = SUBLANES
    nw = nw_ref[...]
    pos = i % tiles_per_seq
    keep_prev = jnp.where(pos != 0, 1.0, 0.0)
    keep_next = jnp.where(pos != tiles_per_seq - 1, 1.0, 0.0)
    hb_ref[0:hl, :] = _rms(xp_ref[...], nw) * keep_prev
    hb_ref[hl:hl + tm, :] = _rms(xc_ref[...], nw)
    hb_ref[hl + tm:, :] = _rms(xn_ref[...], nw) * keep_next
    hbf_ref[...] = hb_ref[...].astype(BF16)

    ck = FFN_COL_CHUNK
    nck = D_FF // ck
    nlb = ck // LANES

    def up(j):
        lo, hi = j * ck, (j + 1) * ck
        uv = jnp.dot(hbf_ref[...], wu_ref[:, lo:hi], preferred_element_type=F32)
        ug = jnp.dot(hbf_ref[...], wu_ref[:, D_FF + lo:D_FF + hi], preferred_element_type=F32)
        for k in range(nlb):
            u_ref[j, k] = uv[:, k * LANES:(k + 1) * LANES]
            u_ref[j, nlb + k] = ug[:, k * LANES:(k + 1) * LANES]

    def conv(j, k, col):
        cs = slice(col, col + LANES)
        return (cw_ref[0:1, cs] * u_ref[j, k, hl - 1:hl - 1 + tm, :]
                + cw_ref[1:2, cs] * u_ref[j, k, hl:hl + tm, :]
                + cw_ref[2:3, cs] * u_ref[j, k, hl + 1:hl + 1 + tm, :]
                + cb_ref[:, cs])

    def act(j):
        for k in range(nlb):
            col = j * ck + k * LANES
            val = conv(j, k, col)
            gate = conv(j, nlb + k, D_FF + col)
            act_ref[:, col:col + LANES] = (gate * jax.nn.sigmoid(gate) * val).astype(BF16)

    up(0)
    for j in range(nck):
        if j + 1 < nck:
            up(j + 1)
        act(j)
    y = xc_ref[...] + jnp.dot(act_ref[...], wd_ref[...], preferred_element_type=F32)
    y_ref[...] = _rms(y, fw_ref[...])


def _ffn(x1, nw, wu, cw, cb, wd, fw, seq_len):
    n = x1.shape[0]
    tm = TOKEN_TILE
    hl = SUBLANES
    per = tm // hl
    nhalo = n // hl
    row = lambda i: (i, 0)
    prev = lambda i: (jnp.maximum(i * per - 1, 0), 0)
    nxt = lambda i: (jnp.minimum((i + 1) * per, nhalo - 1), 0)
    kern = functools.partial(_ffn_kernel, tiles_per_seq=seq_len // tm)
    return pl.pallas_call(
        kern,
        grid=(n // tm,),
        in_specs=[pl.BlockSpec((hl, D_MODEL), prev), pl.BlockSpec((tm, D_MODEL), row),
                  pl.BlockSpec((hl, D_MODEL), nxt),
                  _const_spec((1, D_MODEL)), _const_spec((D_MODEL, 2 * D_FF)),
                  _const_spec((3, 2 * D_FF)), _const_spec((1, 2 * D_FF)),
                  _const_spec((D_FF, D_MODEL)), _const_spec((1, D_MODEL))],
        out_specs=pl.BlockSpec((tm, D_MODEL), row),
        out_shape=jax.ShapeDtypeStruct((n, D_MODEL), F32),
        scratch_shapes=[pltpu.VMEM((tm + 2 * hl, D_MODEL), F32),
                        pltpu.VMEM((tm + 2 * hl, D_MODEL), BF16),
                        pltpu.VMEM((D_FF // FFN_COL_CHUNK, 2 * FFN_COL_CHUNK // LANES, tm + 2 * hl, LANES), F32),
                        pltpu.VMEM((tm, D_FF), BF16)],
        compiler_params=pltpu.CompilerParams(
            dimension_semantics=("arbitrary",), vmem_limit_bytes=VMEM_LIMIT_BYTES),
        name="ffn",
    )(x1, x1, x1, nw, wu, cw, cb, wd, fw)


def _rope_tables(seq_len):
    half = HEAD_DIM // 2
    inv_freq = ROPE_THETA ** (-jnp.arange(half, dtype=F32) / half)
    ang = jnp.arange(seq_len, dtype=F32)[:, None] * inv_freq[None, :]
    cos = jnp.cos(ang)
    sin = jnp.sin(ang)
    reps = LANES // HEAD_DIM
    cos_t = jnp.tile(jnp.concatenate([cos, cos], axis=1), (1, reps))
    sin_t = jnp.tile(jnp.concatenate([-sin, sin], axis=1), (1, reps))
    return cos_t, sin_t


def _trunk(x, p):
    batch, seq_len, _ = x.shape
    n = batch * seq_len
    x2 = x.reshape(n, D_MODEL)
    cos, sin = p["rope"][0][:seq_len], p["rope"][1][:seq_len]
    q, k, v, hq, xf_f, xf_b, hi = _inproj(x2, p["norm_mix_w"], p["w_in"], cos, sin, seq_len)
    of, ob = _hgrn(hq, xf_f, xf_b, hi, p["lbl"], batch, seq_len)
    x1 = _mixattn(p["sink"], x2, q, k, v, of, ob, p["norm_mix_w"], p["w_gate"], p["hgrn_norm_w"], p["w_attn_out"],
                  p["w_hgrn_out"], p["w_mix_out"], batch, seq_len)
    y = _ffn(x1, p["norm_ffn_w"], p["w_up"], p["conv_w"], p["conv_b"], p["w_down"], p["norm_final_w"], seq_len)
    return y.reshape(batch, seq_len, D_MODEL)


def kernel(x_prompt, x_sample, norm_mix_w, w_in, attn_sink, hgrn_lb_logits, hgrn_norm_w, w_attn_out, w_hgrn_out, w_mix_out, norm_ffn_w, w_up, conv_w, conv_b, w_down, norm_final_w):
    p = {
        "rope": _rope_tables(max(x_prompt.shape[1], x_sample.shape[1])),
        "norm_mix_w": norm_mix_w[0].reshape(1, D_MODEL),
        "w_in": w_in[0, :, :_OFF_HG].astype(BF16),
        "w_gate": w_in[0, :, _OFF_HG:].astype(BF16),
        "sink": attn_sink[0].astype(F32),
        "lbl": hgrn_lb_logits.astype(F32).reshape(hgrn_lb_logits.shape[0], 2 * HGRN_WIDTH),
        "hgrn_norm_w": hgrn_norm_w[0].astype(F32).reshape(1, HGRN_WIDTH),
        "w_attn_out": w_attn_out[0].astype(BF16),
        "w_hgrn_out": w_hgrn_out[0].astype(BF16),
        "w_mix_out": w_mix_out[0].astype(BF16),
        "norm_ffn_w": norm_ffn_w[0].reshape(1, D_MODEL),
        "w_up": w_up[0].astype(BF16),
        "conv_w": conv_w[0],
        "conv_b": conv_b[0].reshape(1, 2 * D_FF),
        "w_down": w_down[0].astype(BF16),
        "norm_final_w": norm_final_w.reshape(1, D_MODEL),
    }
    return (_trunk(x_prompt, p), _trunk(x_sample, p))
```

```python
import functools

import jax
import jax.numpy as jnp
from jax import lax
from jax.experimental import pallas as pl
from jax.experimental.pallas import tpu as pltpu

D_MODEL = 1024
HEAD_DIM = 64
N_Q_HEADS = 8
N_KV_HEADS = 2
ATTN_WIDTH = N_Q_HEADS * HEAD_DIM
KV_WIDTH = N_KV_HEADS * HEAD_DIM
WINDOW = 128
ROPE_THETA = 10000.0
HGRN_HEADS = 4
HGRN_DK = 128
HGRN_WIDTH = HGRN_HEADS * HGRN_DK
HGRN_CHUNK = 64
D_FF = 2816
NORM_EPS = 1e-6

_OFF_Q = 0
_OFF_K = _OFF_Q + ATTN_WIDTH
_OFF_V = _OFF_K + KV_WIDTH
_OFF_HQ = _OFF_V + KV_WIDTH
_OFF_HFF = _OFF_HQ + HGRN_WIDTH
_OFF_HFB = _OFF_HFF + HGRN_WIDTH
_OFF_HI = _OFF_HFB + HGRN_WIDTH
_OFF_HG = _OFF_HI + HGRN_WIDTH
_OFF_GA = _OFF_HG + HGRN_WIDTH
_OFF_GH = _OFF_GA + D_MODEL
D_IN = _OFF_GH + D_MODEL

LANES = 128
SUBLANES = 8
VMEM_LIMIT_BYTES = 56 * 1024 * 1024

TOKEN_TILE = 512
INPROJ_TILE = 1024
HGRN_TILE = 1024
ATTN_TILE = 512
FFN_COL_CHUNK = 256

BF16 = jnp.bfloat16
F32 = jnp.float32

_HEAD_PAIR_ORDER = tuple(h for j in range(N_Q_HEADS // 2) for h in (j, j + N_Q_HEADS // 2))


def _const_spec(shape):
    nd = len(shape)
    return pl.BlockSpec(shape, lambda *_: (0,) * nd, pipeline_mode=pl.Buffered(1))


def _rms(x, w):
    ms = jnp.mean(x * x, axis=-1, keepdims=True)
    return x * lax.rsqrt(ms + NORM_EPS) * w


def _rotate_half_pairs(x):
    lane = lax.broadcasted_iota(jnp.int32, x.shape, 1)
    first_half = (lane % HEAD_DIM) < (HEAD_DIM // 2)
    fwd = pltpu.roll(x, LANES - HEAD_DIM // 2, 1)
    bwd = pltpu.roll(x, HEAD_DIM // 2, 1)
    return jnp.where(first_half, fwd, bwd)


def _inproj_kernel(x_ref, nw_ref, w_ref, cos_ref, sin_ref,
                   q_ref, k_ref, v_ref, hq_ref, xff_ref, xfb_ref, hi_ref, h_ref):
    h_ref[...] = _rms(x_ref[...], nw_ref[...]).astype(BF16)

    def proj(lo, hi):
        return jnp.dot(h_ref[...], w_ref[:, lo:hi], preferred_element_type=F32)

    def rope(z):
        return z * cos_ref[...] + _rotate_half_pairs(z) * sin_ref[...]

    zq = proj(_OFF_Q, _OFF_K)
    n_groups = ATTN_WIDTH // LANES
    grp = [rope(zq[:, g * LANES:(g + 1) * LANES]) * (HEAD_DIM ** -0.5) for g in range(n_groups)]
    lo_half = lax.broadcasted_iota(jnp.int32, grp[0].shape, 1) < HEAD_DIM
    swap = lambda z: pltpu.roll(z, HEAD_DIM, 1)
    for g in range(n_groups // 2):
        a, b = grp[g], grp[g + n_groups // 2]
        q_ref[:, 2 * g * LANES:(2 * g + 1) * LANES] = jnp.where(lo_half, a, swap(b)).astype(BF16)
        q_ref[:, (2 * g + 1) * LANES:(2 * g + 2) * LANES] = jnp.where(lo_half, swap(a), b).astype(BF16)
    k_ref[...] = rope(proj(_OFF_K, _OFF_V)).astype(BF16)
    v_ref[...] = proj(_OFF_V, _OFF_HQ).astype(BF16)
    hq_ref[...] = proj(_OFF_HQ, _OFF_HFF)
    xff_ref[...] = proj(_OFF_HFF, _OFF_HFB)
    xfb_ref[...] = proj(_OFF_HFB, _OFF_HI)
    hi_ref[...] = proj(_OFF_HI, _OFF_HG).astype(BF16)


def _inproj(x2, nw, w_in, cos, sin, seq_len):
    n = x2.shape[0]
    tm = INPROJ_TILE
    tiles_per_seq = seq_len // tm
    row = lambda i: (i, 0)
    pos = lambda i: (i % tiles_per_seq, 0)
    tok = lambda w: pl.BlockSpec((tm, w), row)
    widths = (ATTN_WIDTH, KV_WIDTH, KV_WIDTH, HGRN_WIDTH, HGRN_WIDTH, HGRN_WIDTH, HGRN_WIDTH)
    dtypes = (BF16, BF16, BF16, F32, F32, F32, BF16)
    shapes = [jax.ShapeDtypeStruct((n, w), dt) for w, dt in zip(widths, dtypes)]
    return pl.pallas_call(
        _inproj_kernel,
        grid=(n // tm,),
        in_specs=[tok(D_MODEL), _const_spec((1, D_MODEL)), _const_spec(w_in.shape),
                  pl.BlockSpec((tm, LANES), pos), pl.BlockSpec((tm, LANES), pos)],
        out_specs=[tok(w) for w in widths],
        out_shape=shapes,
        scratch_shapes=[pltpu.VMEM((tm, D_MODEL), BF16)],
        compiler_params=pltpu.CompilerParams(
            dimension_semantics=("arbitrary",), vmem_limit_bytes=VMEM_LIMIT_BYTES),
        name="inproj",
    )(x2, nw, w_in, cos, sin)


def _mixattn_kernel(sink_ref, x_ref, q_ref, kp_ref, kc_ref, kn_ref, vp_ref, vc_ref, vn_ref, of_ref, ob_ref,
                    nw_ref, wg_ref, gw_ref, wa_ref, wh_ref, wm_ref, x1_ref,
                    kw_ref, vw_ref, sc_ref, h_ref, as_ref, g_ref):
    h_ref[...] = _rms(x_ref[...], nw_ref[...]).astype(BF16)
    i = pl.program_id(1)
    nblk = pl.num_programs(1)
    tq = q_ref.shape[0]
    w = WINDOW
    kw_ref[0:w, :] = kp_ref[...]
    kw_ref[w:w + tq, :] = kc_ref[...]
    kw_ref[w + tq:, :] = kn_ref[...]
    vw_ref[0:w, :] = vp_ref[...]
    vw_ref[w:w + tq, :] = vc_ref[...]
    vw_ref[w + tq:, :] = vn_ref[...]

    lane = lax.broadcasted_iota(jnp.int32, (w, LANES), 1)
    lo_half = lane < HEAD_DIM
    lane_v = lax.broadcasted_iota(jnp.int32, (3 * w, LANES), 1)
    lo_half_v = lane_v < HEAD_DIM
    r = lax.broadcasted_iota(jnp.int32, (w, w), 0)
    cc = lax.broadcasted_iota(jnp.int32, (w, w), 1)
    band_prev = cc >= r
    band_next = cc <= r
    n_pairs = N_Q_HEADS // 2
    n_sub = tq // w

    def attn_sub(s):
        ok_prev = band_prev if s > 0 else jnp.logical_and(band_prev, i != 0)
        ok_next = band_next if s < n_sub - 1 else jnp.logical_and(band_next, i != nblk - 1)
        kwin = kw_ref[s * w:s * w + 3 * w, :]
        vwin = vw_ref[s * w:s * w + 3 * w, :]
        zero_v = jnp.zeros_like(vwin)
        vblk = jnp.concatenate([jnp.where(lo_half_v, vwin, zero_v),
                                jnp.where(lo_half_v, zero_v, vwin)], axis=0)
        parts = []
        for j in range(n_pairs):
            qj = q_ref[s * w:(s + 1) * w, j * LANES:(j + 1) * LANES]
            zero_q = jnp.zeros_like(qj)
            parts.append(jnp.where(lo_half, qj, zero_q))
            parts.append(jnp.where(lo_half, zero_q, qj))
        lhs = jnp.concatenate(parts, axis=0)
        sc_ref[...] = lax.dot_general(lhs, kwin, (((1,), (1,)), ((), ())),
                                      preferred_element_type=F32)
        for j in range(n_pairs):
            pn = []
            inv = []
            for t in range(2):
                head = j + t * n_pairs
                rows = slice((2 * j + t) * w, (2 * j + t + 1) * w)
                s0 = jnp.where(ok_prev, sc_ref[rows, 0:w], -jnp.inf)
                s1 = sc_ref[rows, w:2 * w]
                s2 = jnp.where(ok_next, sc_ref[rows, 2 * w:3 * w], -jnp.inf)
                sink = sink_ref[head]
                m = jnp.max(jnp.maximum(jnp.maximum(s0, s1), s2), axis=-1, keepdims=True)
                m = jnp.maximum(m, sink)
                p0 = jnp.exp(s0 - m)
                p1 = jnp.exp(s1 - m)
                p2 = jnp.exp(s2 - m)
                denom = jnp.sum(p0 + p1 + p2, axis=-1, keepdims=True) + jnp.exp(sink - m)
                inv.append(1.0 / denom)
                pn += [p0.astype(BF16), p1.astype(BF16), p2.astype(BF16)]
            pcat = jnp.concatenate(pn, axis=1)
            o = jnp.dot(pcat, vblk, preferred_element_type=F32)
            o = o * jnp.where(lo_half, inv[0], inv[1])
            as_ref[s * w:(s + 1) * w, j * LANES:(j + 1) * LANES] = o.astype(BF16)

    piece = D_MODEL // 2
    n_piece = (D_IN - _OFF_HG) // piece
    for u in range(max(n_piece, n_sub)):
        if u < n_piece:
            g_ref[:, u * piece:(u + 1) * piece] = jnp.dot(h_ref[...], wg_ref[:, u * piece:(u + 1) * piece],
                                                          preferred_element_type=F32)
        if u < n_sub:
            attn_sub(u)

    def gate(lo, hi):
        return g_ref[:, lo - _OFF_HG:hi - _OFF_HG]

    o = of_ref[...].astype(F32) + ob_ref[...].astype(F32)
    hg = gate(_OFF_HG, _OFF_GA)
    gw = gw_ref[...]
    parts = []
    for hd in range(HGRN_HEADS):
        cols = slice(hd * HGRN_DK, (hd + 1) * HGRN_DK)
        oh = o[:, cols]
        ms = jnp.mean(oh * oh, axis=-1, keepdims=True)
        parts.append(oh * lax.rsqrt(ms + NORM_EPS))
    on = jnp.concatenate(parts, axis=1)
    rr = (on * gw * (hg * jax.nn.sigmoid(hg))).astype(BF16)
    wa = jnp.concatenate([wa_ref[h * HEAD_DIM:(h + 1) * HEAD_DIM, :] for h in _HEAD_PAIR_ORDER], axis=0)
    pa = jnp.dot(as_ref[...], wa, preferred_element_type=F32)
    ph = jnp.dot(rr, wh_ref[...], preferred_element_type=F32)
    mixed = (jax.nn.sigmoid(gate(_OFF_GA, _OFF_GH)) * pa
             + jax.nn.sigmoid(gate(_OFF_GH, D_IN)) * ph).astype(BF16)
    x1_ref[...] = x_ref[...] + jnp.dot(mixed, wm_ref[...], preferred_element_type=F32)


def _mixattn(sink, x2, q, k, v, of, ob, nw, wg, gw, wa, wh, wm, batch, seq_len):
    tq = ATTN_TILE
    w = WINDOW
    nblk = seq_len // tq
    sub = tq // w
    n = x2.shape[0]
    nrow_blocks = batch * seq_len // w
    cur = lambda b, i, *_: (b * nblk + i, 0)
    prev = lambda b, i, *_: (jnp.maximum((b * nblk + i) * sub - 1, 0), 0)
    nxt = lambda b, i, *_: (jnp.minimum((b * nblk + i + 1) * sub, nrow_blocks - 1), 0)
    kv_cur = pl.BlockSpec((tq, KV_WIDTH), cur)
    kv_prev = pl.BlockSpec((w, KV_WIDTH), prev)
    kv_next = pl.BlockSpec((w, KV_WIDTH), nxt)
    tok = lambda wd: pl.BlockSpec((tq, wd), cur)
    return pl.pallas_call(
        _mixattn_kernel,
        grid_spec=pltpu.PrefetchScalarGridSpec(
            num_scalar_prefetch=1,
            grid=(batch, nblk),
            in_specs=[tok(D_MODEL), tok(ATTN_WIDTH), kv_prev, kv_cur, kv_next, kv_prev, kv_cur, kv_next,
                      tok(HGRN_WIDTH), tok(HGRN_WIDTH),
                      _const_spec((1, D_MODEL)), _const_spec(wg.shape),
                      _const_spec((1, HGRN_WIDTH)), _const_spec((ATTN_WIDTH, D_MODEL)),
                      _const_spec((HGRN_WIDTH, D_MODEL)), _const_spec((D_MODEL, D_MODEL))],
            out_specs=tok(D_MODEL),
            scratch_shapes=[pltpu.VMEM((tq + 2 * w, KV_WIDTH), BF16),
                            pltpu.VMEM((tq + 2 * w, KV_WIDTH), BF16),
                            pltpu.VMEM((N_Q_HEADS * w, 3 * w), F32),
                            pltpu.VMEM((tq, D_MODEL), BF16),
                            pltpu.VMEM((tq, ATTN_WIDTH), BF16),
                            pltpu.VMEM((tq, D_IN - _OFF_HG), F32)],
        ),
        out_shape=jax.ShapeDtypeStruct((n, D_MODEL), F32),
        compiler_params=pltpu.CompilerParams(
            dimension_semantics=("arbitrary", "arbitrary"), vmem_limit_bytes=VMEM_LIMIT_BYTES),
        name="mixattn",
    )(sink, x2, q, k, k, k, v, v, v, of, ob, nw, wg, gw, wa, wh, wm)


def _hgrn_kernel(hq_f_ref, xf_f_ref, v_f_ref, hq_b_ref, xf_b_ref, v_b_ref, lbl_ref,
                 of_ref, ob_ref,
                 s_ref, sbf_ref, p_ref, dc_ref, a_ref, u_ref, gp_ref, kk_ref, b_ref):
    @pl.when(pl.program_id(1) == 0)
    def _():
        s_ref[...] = jnp.zeros_like(s_ref)
        sbf_ref[...] = jnp.zeros_like(sbf_ref)

    c = HGRN_CHUNK
    n_chunks = of_ref.shape[0] // c
    r = lax.broadcasted_iota(jnp.int32, (c, c), 0)
    cc = lax.broadcasted_iota(jnp.int32, (c, c), 1)
    keep = (r >= cc, r <= cc)
    tri = (keep[0].astype(BF16), keep[1].astype(BF16))
    nt = (((1,), (1,)), ((), ()))
    tn = (((0,), (0,)), ((), ()))
    ins = ((hq_f_ref, xf_f_ref, v_f_ref, of_ref), (hq_b_ref, xf_b_ref, v_b_ref, ob_ref))

    lbl = lbl_ref[...]
    e = jnp.exp(lbl - jnp.max(lbl, axis=0, keepdims=True))
    lb = e[0:1, :] / jnp.sum(e, axis=0, keepdims=True)

    def chunk_rows(d, t):
        ci = t if d == 0 else n_chunks - 1 - t
        return slice(ci * c, (ci + 1) * c)

    def stage1(t, d):
        wide = slice(d * HGRN_WIDTH, (d + 1) * HGRN_WIDTH)
        xf = ins[d][1][chunk_rows(d, t), :]
        lbd = lb[:, wide]
        f = lbd + (1.0 - lbd) * jax.nn.sigmoid(xf)
        kk_ref[:, wide] = 1.0 - f
        g = jnp.log(f)
        g1 = g.astype(BF16)
        r1 = g - g1.astype(F32)
        g2 = r1.astype(BF16)
        gp_ref[0, :, wide] = g1
        gp_ref[1, :, wide] = g2
        gp_ref[2, :, wide] = (r1 - g2.astype(F32)).astype(BF16)

    def cumsum(d):
        wide = slice(d * HGRN_WIDTH, (d + 1) * HGRN_WIDTH)
        b_ref[:, wide] = (jnp.dot(tri[d], gp_ref[0, :, wide], preferred_element_type=F32)
                          + jnp.dot(tri[d], gp_ref[1, :, wide], preferred_element_type=F32)
                          + jnp.dot(tri[d], gp_ref[2, :, wide], preferred_element_type=F32))

    def stage3(slot, t, d):
        wide = slice(d * HGRN_WIDTH, (d + 1) * HGRN_WIDTH)
        hq = ins[d][0][chunk_rows(d, t), :]
        b = b_ref[:, wide]
        if d == 0:
            ref = b[c // 2 - 1:c // 2, :]
            b_last = b[c - 1:c, :]
        else:
            ref = b[c // 2:c // 2 + 1, :]
            b_last = b[0:1, :]
        qd = hq * jnp.exp(b - ref)
        kd = kk_ref[:, wide] * jnp.exp(ref - b)
        staged = (qd, kd, qd * jnp.exp(ref), kd * jnp.exp(b_last - ref))
        for kind, val in enumerate(staged):
            for hd in range(HGRN_HEADS):
                p_ref[slot, d, kind, hd] = val[:, hd * HGRN_DK:(hd + 1) * HGRN_DK].astype(BF16)
        dc_ref[slot, d, 0:1, :] = jnp.exp(b_last)

    units = [(d, hd) for d in range(2) for hd in range(HGRN_HEADS)]

    def step(slot, t, prep_next):
        if prep_next:
            stage1(t + 1, 0)
            stage1(t + 1, 1)
        for d, hd in units:
            a = lax.dot_general(p_ref[slot, d, 0, hd], p_ref[slot, d, 1, hd], nt,
                                preferred_element_type=F32)
            a_ref[d * HGRN_HEADS + hd] = jnp.where(keep[d], a, 0.0).astype(BF16)
        if prep_next:
            cumsum(0)
            cumsum(1)
        for d in range(2):
            v_all = ins[d][2][chunk_rows(d, t), :]
            for hd in range(HGRN_HEADS):
                vh = v_all[:, hd * HGRN_DK:(hd + 1) * HGRN_DK]
                u_ref[d * HGRN_HEADS + hd] = lax.dot_general(vh, p_ref[slot, d, 3, hd], tn,
                                                             preferred_element_type=F32)
        for d in range(2):
            rows = chunk_rows(d, t)
            v_all = ins[d][2][rows, :]
            outs = []
            for hd in range(HGRN_HEADS):
                idx = d * HGRN_HEADS + hd
                o = lax.dot_general(p_ref[slot, d, 2, hd], sbf_ref[idx], nt, preferred_element_type=F32)
                o = o + jnp.dot(a_ref[idx], v_all[:, hd * HGRN_DK:(hd + 1) * HGRN_DK],
                                preferred_element_type=F32)
                outs.append(o.astype(ins[d][3].dtype))
            ins[d][3][rows, :] = jnp.concatenate(outs, axis=1)
        if prep_next:
            stage3(1 - slot, t + 1, 0)
            stage3(1 - slot, t + 1, 1)
        for d, hd in units:
            cols = slice(hd * HGRN_DK, (hd + 1) * HGRN_DK)
            idx = d * HGRN_HEADS + hd
            st = s_ref[idx] * dc_ref[slot, d, 0:1, cols] + u_ref[idx]
            s_ref[idx] = st
            sbf_ref[idx] = st.astype(BF16)

    for d in range(2):
        stage1(0, d)
    for d in range(2):
        cumsum(d)
    for d in range(2):
        stage3(0, 0, d)
    for t in range(n_chunks):
        step(t % 2, t, t + 1 < n_chunks)


def _hgrn(hq, xf_f, xf_b, v, lbl, batch, seq_len):
    tm = HGRN_TILE
    nblk = seq_len // tm
    n = v.shape[0]
    w = HGRN_WIDTH
    fwd = pl.BlockSpec((tm, w), lambda b, i: (b * nblk + i, 0))
    bwd = pl.BlockSpec((tm, w), lambda b, i: (b * nblk + nblk - 1 - i, 0))
    out = jax.ShapeDtypeStruct((n, w), BF16)
    nu = 2 * HGRN_HEADS
    return pl.pallas_call(
        _hgrn_kernel,
        grid=(batch, nblk),
        in_specs=[fwd, fwd, fwd, bwd, bwd, bwd, _const_spec(lbl.shape)],
        out_specs=[fwd, bwd],
        out_shape=[out, out],
        scratch_shapes=[pltpu.VMEM((nu, HGRN_DK, HGRN_DK), F32),
                        pltpu.VMEM((nu, HGRN_DK, HGRN_DK), BF16),
                        pltpu.VMEM((2, 2, 4, HGRN_HEADS, HGRN_CHUNK, HGRN_DK), BF16),
                        pltpu.VMEM((2, 2, SUBLANES, w), F32),
                        pltpu.VMEM((nu, HGRN_CHUNK, HGRN_CHUNK), BF16),
                        pltpu.VMEM((nu, HGRN_DK, HGRN_DK), F32),
                        pltpu.VMEM((3, HGRN_CHUNK, 2 * w), BF16),
                        pltpu.VMEM((HGRN_CHUNK, 2 * w), F32),
                        pltpu.VMEM((HGRN_CHUNK, 2 * w), F32)],
        compiler_params=pltpu.CompilerParams(
            dimension_semantics=("arbitrary", "arbitrary"), vmem_limit_bytes=VMEM_LIMIT_BYTES),
        name="hgrn",
    )(hq, xf_f, v, hq, xf_b, v, lbl)


def _ffn_kernel(xp_ref, xc_ref, xn_ref, nw_ref, wu_ref, cw_ref, cb_ref, wd_ref, fw_ref,
                y_ref, hb_ref, hbf_ref, u_ref, act_ref, *, tiles_per_seq):
    i = pl.program_id(0)
    tm = xc_ref.shape[0]
    hl = SUBLANES
    nw = nw_ref[...]
    pos = i % tiles_per_seq
    keep_prev = jnp.where(pos != 0, 1.0, 0.0)
    keep_next = jnp.where(pos != tiles_per_seq - 1, 1.0, 0.0)
    hb_ref[0:hl, :] = _rms(xp_ref[...], nw) * keep_prev
    hb_ref[hl:hl + tm, :] = _rms(xc_ref[...], nw)
    hb_ref[hl + tm:, :] = _rms(xn_ref[...], nw) * keep_next
    hbf_ref[...] = hb_ref[...].astype(BF16)

    ck = FFN_COL_CHUNK
    nck = D_FF // ck
    nlb = ck // LANES

    def up(j):
        lo, hi = j * ck, (j + 1) * ck
        uv = jnp.dot(hbf_ref[...], wu_ref[:, lo:hi], preferred_element_type=F32)
        ug = jnp.dot(hbf_ref[...], wu_ref[:, D_FF + lo:D_FF + hi], preferred_element_type=F32)
        for k in range(nlb):
            u_ref[j, k] = uv[:, k * LANES:(k + 1) * LANES]
            u_ref[j, nlb + k] = ug[:, k * LANES:(k + 1) * LANES]

    def conv(j, k, col):
        cs = slice(col, col + LANES)
        return (cw_ref[0:1, cs] * u_ref[j, k, hl - 1:hl - 1 + tm, :]
                + cw_ref[1:2, cs] * u_ref[j, k, hl:hl + tm, :]
                + cw_ref[2:3, cs] * u_ref[j, k, hl + 1:hl + 1 + tm, :]
                + cb_ref[:, cs])

    def act(j):
        for k in range(nlb):
            col = j * ck + k * LANES
            val = conv(j, k, col)
            gate = conv(j, nlb + k, D_FF + col)
            act_ref[:, col:col + LANES] = (gate * jax.nn.sigmoid(gate) * val).astype(BF16)

    up(0)
    for j in range(nck):
        if j + 1 < nck:
            up(j + 1)
        act(j)
    y = xc_ref[...] + jnp.dot(act_ref[...], wd_ref[...], preferred_element_type=F32)
    y_ref[...] = _rms(y, fw_ref[...])


def _ffn(x1, nw, wu, cw, cb, wd, fw, seq_len):
    n = x1.shape[0]
    tm = TOKEN_TILE
    hl = SUBLANES
    per = tm // hl
    nhalo = n // hl
    row = lambda i: (i, 0)
    prev = lambda i: (jnp.maximum(i * per - 1, 0), 0)
    nxt = lambda i: (jnp.minimum((i + 1) * per, nhalo - 1), 0)
    kern = functools.partial(_ffn_kernel, tiles_per_seq=seq_len // tm)
    return pl.pallas_call(
        kern,
        grid=(n // tm,),
        in_specs=[pl.BlockSpec((hl, D_MODEL), prev), pl.BlockSpec((tm, D_MODEL), row),
                  pl.BlockSpec((hl, D_MODEL), nxt),
                  _const_spec((1, D_MODEL)), _const_spec((D_MODEL, 2 * D_FF)),
                  _const_spec((3, 2 * D_FF)), _const_spec((1, 2 * D_FF)),
                  _const_spec((D_FF, D_MODEL)), _const_spec((1, D_MODEL))],
        out_specs=pl.BlockSpec((tm, D_MODEL), row),
        out_shape=jax.ShapeDtypeStruct((n, D_MODEL), F32),
        scratch_shapes=[pltpu.VMEM((tm + 2 * hl, D_MODEL), F32),
                        pltpu.VMEM((tm + 2 * hl, D_MODEL), BF16),
                        pltpu.VMEM((D_FF // FFN_COL_CHUNK, 2 * FFN_COL_CHUNK // LANES, tm + 2 * hl, LANES), F32),
                        pltpu.VMEM((tm, D_FF), BF16)],
        compiler_params=pltpu.CompilerParams(
            dimension_semantics=("arbitrary",), vmem_limit_bytes=VMEM_LIMIT_BYTES),
        name="ffn",
    )(x1, x1, x1, nw, wu, cw, cb, wd, fw)


def _rope_tables(seq_len):
    half = HEAD_DIM // 2
    inv_freq = ROPE_THETA ** (-jnp.arange(half, dtype=F32) / half)
    ang = jnp.arange(seq_len, dtype=F32)[:, None] * inv_freq[None, :]
    cos = jnp.cos(ang)
    sin = jnp.sin(ang)
    reps = LANES // HEAD_DIM
    cos_t = jnp.tile(jnp.concatenate([cos, cos], axis=1), (1, reps))
    sin_t = jnp.tile(jnp.concatenate([-sin, sin], axis=1), (1, reps))
    return cos_t, sin_t


def _trunk(x, p):
    batch, seq_len, _ = x.shape
    n = batch * seq_len
    x2 = x.reshape(n, D_MODEL)
    cos, sin = p["rope"][0][:seq_len], p["rope"][1][:seq_len]
    q, k, v, hq, xf_f, xf_b, hi = _inproj(x2, p["norm_mix_w"], p["w_in"], cos, sin, seq_len)
    of, ob = _hgrn(hq, xf_f, xf_b, hi, p["lbl"], batch, seq_len)
    x1 = _mixattn(p["sink"], x2, q, k, v, of, ob, p["norm_mix_w"], p["w_gate"], p["hgrn_norm_w"], p["w_attn_out"],
                  p["w_hgrn_out"], p["w_mix_out"], batch, seq_len)
    y = _ffn(x1, p["norm_ffn_w"], p["w_up"], p["conv_w"], p["conv_b"], p["w_down"], p["norm_final_w"], seq_len)
    return y.reshape(batch, seq_len, D_MODEL)


def kernel(x_prompt, x_sample, norm_mix_w, w_in, attn_sink, hgrn_lb_logits, hgrn_norm_w, w_attn_out, w_hgrn_out, w_mix_out, norm_ffn_w, w_up, conv_w, conv_b, w_down, norm_final_w):
    p = {
        "rope": _rope_tables(max(x_prompt.shape[1], x_sample.shape[1])),
        "norm_mix_w": norm_mix_w[0].reshape(1, D_MODEL),
        "w_in": w_in[0, :, :_OFF_HG].astype(BF16),
        "w_gate": w_in[0, :, _OFF_HG:].astype(BF16),
        "sink": attn_sink[0].astype(F32),
        "lbl": hgrn_lb_logits.astype(F32).reshape(hgrn_lb_logits.shape[0], 2 * HGRN_WIDTH),
        "hgrn_norm_w": hgrn_norm_w[0].astype(F32).reshape(1, HGRN_WIDTH),
        "w_attn_out": w_attn_out[0].astype(BF16),
        "w_hgrn_out": w_hgrn_out[0].astype(BF16),
        "w_mix_out": w_mix_out[0].astype(BF16),
        "norm_ffn_w": norm_ffn_w[0].reshape(1, D_MODEL),
        "w_up": w_up[0].astype(BF16),
        "conv_w": conv_w[0],
        "conv_b": conv_b[0].reshape(1, 2 * D_FF),
        "w_down": w_down[0].astype(BF16),
        "norm_final_w": norm_final_w.reshape(1, D_MODEL),
    }
    return (_trunk(x_prompt, p), _trunk(x_sample, p))
```

```python
import functools

import jax
import jax.numpy as jnp
from jax import lax
from jax.experimental import pallas as pl
from jax.experimental.pallas import tpu as pltpu

D_MODEL = 1024
HEAD_DIM = 64
N_Q_HEADS = 8
N_KV_HEADS = 2
ATTN_WIDTH = N_Q_HEADS * HEAD_DIM
KV_WIDTH = N_KV_HEADS * HEAD_DIM
WINDOW = 128
ROPE_THETA = 10000.0
HGRN_HEADS = 4
HGRN_DK = 128
HGRN_WIDTH = HGRN_HEADS * HGRN_DK
HGRN_CHUNK = 64
D_FF = 2816
NORM_EPS = 1e-6

_OFF_Q = 0
_OFF_K = _OFF_Q + ATTN_WIDTH
_OFF_V = _OFF_K + KV_WIDTH
_OFF_HQ = _OFF_V + KV_WIDTH
_OFF_HFF = _OFF_HQ + HGRN_WIDTH
_OFF_HFB = _OFF_HFF + HGRN_WIDTH
_OFF_HI = _OFF_HFB + HGRN_WIDTH
_OFF_HG = _OFF_HI + HGRN_WIDTH
_OFF_GA = _OFF_HG + HGRN_WIDTH
_OFF_GH = _OFF_GA + D_MODEL
D_IN = _OFF_GH + D_MODEL

LANES = 128
SUBLANES = 8
VMEM_LIMIT_BYTES = 56 * 1024 * 1024

TOKEN_TILE = 512
INPROJ_TILE = 1024
HGRN_TILE = 1024
ATTN_TILE = 512
FFN_COL_CHUNK = 256

BF16 = jnp.bfloat16
F32 = jnp.float32

_HEAD_PAIR_ORDER = tuple(h for j in range(N_Q_HEADS // 2) for h in (j, j + N_Q_HEADS // 2))


def _const_spec(shape):
    nd = len(shape)
    return pl.BlockSpec(shape, lambda *_: (0,) * nd, pipeline_mode=pl.Buffered(1))


def _rms(x, w):
    ms = jnp.mean(x * x, axis=-1, keepdims=True)
    return x * lax.rsqrt(ms + NORM_EPS) * w


def _rotate_half_pairs(x):
    lane = lax.broadcasted_iota(jnp.int32, x.shape, 1)
    first_half = (lane % HEAD_DIM) < (HEAD_DIM // 2)
    fwd = pltpu.roll(x, LANES - HEAD_DIM // 2, 1)
    bwd = pltpu.roll(x, HEAD_DIM // 2, 1)
    return jnp.where(first_half, fwd, bwd)


def _inproj_kernel(x_ref, nw_ref, w_ref, cos_ref, sin_ref,
                   q_ref, k_ref, v_ref, hq_ref, xff_ref, xfb_ref, hi_ref, h_ref):
    h_ref[...] = _rms(x_ref[...], nw_ref[...]).astype(BF16)

    def proj(lo, hi):
        return jnp.dot(h_ref[...], w_ref[:, lo:hi], preferred_element_type=F32)

    def rope(z):
        return z * cos_ref[...] + _rotate_half_pairs(z) * sin_ref[...]

    zq = proj(_OFF_Q, _OFF_K)
    n_groups = ATTN_WIDTH // LANES
    grp = [rope(zq[:, g * LANES:(g + 1) * LANES]) * (HEAD_DIM ** -0.5) for g in range(n_groups)]
    lo_half = lax.broadcasted_iota(jnp.int32, grp[0].shape, 1) < HEAD_DIM
    swap = lambda z: pltpu.roll(z, HEAD_DIM, 1)
    for g in range(n_groups // 2):
        a, b = grp[g], grp[g + n_groups // 2]
        q_ref[:, 2 * g * LANES:(2 * g + 1) * LANES] = jnp.where(lo_half, a, swap(b)).astype(BF16)
        q_ref[:, (2 * g + 1) * LANES:(2 * g + 2) * LANES] = jnp.where(lo_half, swap(a), b).astype(BF16)
    k_ref[...] = rope(proj(_OFF_K, _OFF_V)).astype(BF16)
    v_ref[...] = proj(_OFF_V, _OFF_HQ).astype(BF16)
    hq_ref[...] = proj(_OFF_HQ, _OFF_HFF)
    xff_ref[...] = proj(_OFF_HFF, _OFF_HFB)
    xfb_ref[...] = proj(_OFF_HFB, _OFF_HI)
    hi_ref[...] = proj(_OFF_HI, _OFF_HG).astype(BF16)


def _inproj(x2, nw, w_in, cos, sin, seq_len):
    n = x2.shape[0]
    tm = INPROJ_TILE
    tiles_per_seq = seq_len // tm
    row = lambda i: (i, 0)
    pos = lambda i: (i % tiles_per_seq, 0)
    tok = lambda w: pl.BlockSpec((tm, w), row)
    widths = (ATTN_WIDTH, KV_WIDTH, KV_WIDTH, HGRN_WIDTH, HGRN_WIDTH, HGRN_WIDTH, HGRN_WIDTH)
    dtypes = (BF16, BF16, BF16, F32, F32, F32, BF16)
    shapes = [jax.ShapeDtypeStruct((n, w), dt) for w, dt in zip(widths, dtypes)]
    return pl.pallas_call(
        _inproj_kernel,
        grid=(n // tm,),
        in_specs=[tok(D_MODEL), _const_spec((1, D_MODEL)), _const_spec(w_in.shape),
                  pl.BlockSpec((tm, LANES), pos), pl.BlockSpec((tm, LANES), pos)],
        out_specs=[tok(w) for w in widths],
        out_shape=shapes,
        scratch_shapes=[pltpu.VMEM((tm, D_MODEL), BF16)],
        compiler_params=pltpu.CompilerParams(
            dimension_semantics=("arbitrary",), vmem_limit_bytes=VMEM_LIMIT_BYTES),
        name="inproj",
    )(x2, nw, w_in, cos, sin)


def _mixattn_kernel(sink_ref, x_ref, q_ref, kp_ref, kc_ref, kn_ref, vp_ref, vc_ref, vn_ref, of_ref, ob_ref,
                    nw_ref, wg_ref, gw_ref, wa_ref, wh_ref, wm_ref, x1_ref,
                    kw_ref, vw_ref, sc_ref, h_ref, as_ref, g_ref):
    h_ref[...] = _rms(x_ref[...], nw_ref[...]).astype(BF16)
    i = pl.program_id(1)
    nblk = pl.num_programs(1)
    tq = q_ref.shape[0]
    w = WINDOW
    kw_ref[0:w, :] = kp_ref[...]
    kw_ref[w:w + tq, :] = kc_ref[...]
    kw_ref[w + tq:, :] = kn_ref[...]
    vw_ref[0:w, :] = vp_ref[...]
    vw_ref[w:w + tq, :] = vc_ref[...]
    vw_ref[w + tq:, :] = vn_ref[...]

    lane = lax.broadcasted_iota(jnp.int32, (w, LANES), 1)
    lo_half = lane < HEAD_DIM
    lane_v = lax.broadcasted_iota(jnp.int32, (3 * w, LANES), 1)
    lo_half_v = lane_v < HEAD_DIM
    r = lax.broadcasted_iota(jnp.int32, (w, w), 0)
    cc = lax.broadcasted_iota(jnp.int32, (w, w), 1)
    band_prev = cc >= r
    band_next = cc <= r
    n_pairs = N_Q_HEADS // 2
    n_sub = tq // w

    def attn_sub(s):
        ok_prev = band_prev if s > 0 else jnp.logical_and(band_prev, i != 0)
        ok_next = band_next if s < n_sub - 1 else jnp.logical_and(band_next, i != nblk - 1)
        kwin = kw_ref[s * w:s * w + 3 * w, :]
        vwin = vw_ref[s * w:s * w + 3 * w, :]
        zero_v = jnp.zeros_like(vwin)
        vblk = jnp.concatenate([jnp.where(lo_half_v, vwin, zero_v),
                                jnp.where(lo_half_v, zero_v, vwin)], axis=0)
        parts = []
        q_rows = q_ref[s * w:(s + 1) * w, :]
        for j in range(n_pairs):
            qj = q_rows[:, j * LANES:(j + 1) * LANES]
            zero_q = jnp.zeros_like(qj)
            parts.append(jnp.where(lo_half, qj, zero_q))
            parts.append(jnp.where(lo_half, zero_q, qj))
        lhs = jnp.concatenate(parts, axis=0)
        sc_ref[...] = lax.dot_general(lhs, kwin, (((1,), (1,)), ((), ())),
                                      preferred_element_type=F32)
        for j in range(n_pairs):
            pn = []
            inv = []
            for t in range(2):
                head = j + t * n_pairs
                rows = slice((2 * j + t) * w, (2 * j + t + 1) * w)
                s0 = jnp.where(ok_prev, sc_ref[rows, 0:w], -jnp.inf)
                s1 = sc_ref[rows, w:2 * w]
                s2 = jnp.where(ok_next, sc_ref[rows, 2 * w:3 * w], -jnp.inf)
                sink = sink_ref[head]
                m = jnp.max(jnp.maximum(jnp.maximum(s0, s1), s2), axis=-1, keepdims=True)
                m = jnp.maximum(m, sink)
                p0 = jnp.exp(s0 - m)
                p1 = jnp.exp(s1 - m)
                p2 = jnp.exp(s2 - m)
                denom = jnp.sum(p0 + p1 + p2, axis=-1, keepdims=True) + jnp.exp(sink - m)
                inv.append(1.0 / denom)
                pn += [p0.astype(BF16), p1.astype(BF16), p2.astype(BF16)]
            pcat = jnp.concatenate(pn, axis=1)
            o = jnp.dot(pcat, vblk, preferred_element_type=F32)
            o = o * jnp.where(lo_half, inv[0], inv[1])
            as_ref[s * w:(s + 1) * w, j * LANES:(j + 1) * LANES] = o.astype(BF16)

    piece = D_MODEL // 2
    n_piece = (D_IN - _OFF_HG) // piece
    for u in range(max(n_piece, n_sub)):
        if u < n_piece:
            g_ref[:, u * piece:(u + 1) * piece] = jnp.dot(h_ref[...], wg_ref[:, u * piece:(u + 1) * piece],
                                                          preferred_element_type=F32)
        if u < n_sub:
            attn_sub(u)

    def gate(lo, hi):
        return g_ref[:, lo - _OFF_HG:hi - _OFF_HG]

    o = of_ref[...].astype(F32) + ob_ref[...].astype(F32)
    hg = gate(_OFF_HG, _OFF_GA)
    gw = gw_ref[...]
    parts = []
    for hd in range(HGRN_HEADS):
        cols = slice(hd * HGRN_DK, (hd + 1) * HGRN_DK)
        oh = o[:, cols]
        ms = jnp.mean(oh * oh, axis=-1, keepdims=True)
        parts.append(oh * lax.rsqrt(ms + NORM_EPS))
    on = jnp.concatenate(parts, axis=1)
    rr = (on * gw * (hg * jax.nn.sigmoid(hg))).astype(BF16)
    wa = jnp.concatenate([wa_ref[h * HEAD_DIM:(h + 1) * HEAD_DIM, :] for h in _HEAD_PAIR_ORDER], axis=0)
    pa = jnp.dot(as_ref[...], wa, preferred_element_type=F32)
    ph = jnp.dot(rr, wh_ref[...], preferred_element_type=F32)
    mixed = (jax.nn.sigmoid(gate(_OFF_GA, _OFF_GH)) * pa
             + jax.nn.sigmoid(gate(_OFF_GH, D_IN)) * ph).astype(BF16)
    x1_ref[...] = x_ref[...] + jnp.dot(mixed, wm_ref[...], preferred_element_type=F32)


def _mixattn(sink, x2, q, k, v, of, ob, nw, wg, gw, wa, wh, wm, batch, seq_len):
    tq = ATTN_TILE
    w = WINDOW
    nblk = seq_len // tq
    sub = tq // w
    n = x2.shape[0]
    nrow_blocks = batch * seq_len // w
    cur = lambda b, i, *_: (b * nblk + i, 0)
    prev = lambda b, i, *_: (jnp.maximum((b * nblk + i) * sub - 1, 0), 0)
    nxt = lambda b, i, *_: (jnp.minimum((b * nblk + i + 1) * sub, nrow_blocks - 1), 0)
    kv_cur = pl.BlockSpec((tq, KV_WIDTH), cur)
    kv_prev = pl.BlockSpec((w, KV_WIDTH), prev)
    kv_next = pl.BlockSpec((w, KV_WIDTH), nxt)
    tok = lambda wd: pl.BlockSpec((tq, wd), cur)
    return pl.pallas_call(
        _mixattn_kernel,
        grid_spec=pltpu.PrefetchScalarGridSpec(
            num_scalar_prefetch=1,
            grid=(batch, nblk),
            in_specs=[tok(D_MODEL), tok(ATTN_WIDTH), kv_prev, kv_cur, kv_next, kv_prev, kv_cur, kv_next,
                      tok(HGRN_WIDTH), tok(HGRN_WIDTH),
                      _const_spec((1, D_MODEL)), _const_spec(wg.shape),
                      _const_spec((1, HGRN_WIDTH)), _const_spec((ATTN_WIDTH, D_MODEL)),
                      _const_spec((HGRN_WIDTH, D_MODEL)), _const_spec((D_MODEL, D_MODEL))],
            out_specs=tok(D_MODEL),
            scratch_shapes=[pltpu.VMEM((tq + 2 * w, KV_WIDTH), BF16),
                            pltpu.VMEM((tq + 2 * w, KV_WIDTH), BF16),
                            pltpu.VMEM((N_Q_HEADS * w, 3 * w), F32),
                            pltpu.VMEM((tq, D_MODEL), BF16),
                            pltpu.VMEM((tq, ATTN_WIDTH), BF16),
                            pltpu.VMEM((tq, D_IN - _OFF_HG), F32)],
        ),
        out_shape=jax.ShapeDtypeStruct((n, D_MODEL), F32),
        compiler_params=pltpu.CompilerParams(
            dimension_semantics=("arbitrary", "arbitrary"), vmem_limit_bytes=VMEM_LIMIT_BYTES),
        name="mixattn",
    )(sink, x2, q, k, k, k, v, v, v, of, ob, nw, wg, gw, wa, wh, wm)


def _hgrn_kernel(hq_f_ref, xf_f_ref, v_f_ref, hq_b_ref, xf_b_ref, v_b_ref, lbl_ref,
                 of_ref, ob_ref,
                 s_ref, sbf_ref, p_ref, dc_ref, a_ref, u_ref, gp_ref, kk_ref, b_ref):
    @pl.when(pl.program_id(1) == 0)
    def _():
        s_ref[...] = jnp.zeros_like(s_ref)
        sbf_ref[...] = jnp.zeros_like(sbf_ref)

    c = HGRN_CHUNK
    n_chunks = of_ref.shape[0] // c
    r = lax.broadcasted_iota(jnp.int32, (c, c), 0)
    cc = lax.broadcasted_iota(jnp.int32, (c, c), 1)
    keep = (r >= cc, r <= cc)
    tri = (keep[0].astype(BF16), keep[1].astype(BF16))
    nt = (((1,), (1,)), ((), ()))
    tn = (((0,), (0,)), ((), ()))
    ins = ((hq_f_ref, xf_f_ref, v_f_ref, of_ref), (hq_b_ref, xf_b_ref, v_b_ref, ob_ref))

    lbl = lbl_ref[...]
    e = jnp.exp(lbl - jnp.max(lbl, axis=0, keepdims=True))
    lb = e[0:1, :] / jnp.sum(e, axis=0, keepdims=True)

    def chunk_rows(d, t):
        ci = t if d == 0 else n_chunks - 1 - t
        return slice(ci * c, (ci + 1) * c)

    def stage1(t, d):
        wide = slice(d * HGRN_WIDTH, (d + 1) * HGRN_WIDTH)
        xf = ins[d][1][chunk_rows(d, t), :]
        lbd = lb[:, wide]
        f = lbd + (1.0 - lbd) * jax.nn.sigmoid(xf)
        kk_ref[:, wide] = 1.0 - f
        g = jnp.log(f)
        g1 = g.astype(BF16)
        r1 = g - g1.astype(F32)
        g2 = r1.astype(BF16)
        gp_ref[0, :, wide] = g1
        gp_ref[1, :, wide] = g2
        gp_ref[2, :, wide] = (r1 - g2.astype(F32)).astype(BF16)

    def cumsum(d):
        wide = slice(d * HGRN_WIDTH, (d + 1) * HGRN_WIDTH)
        b_ref[:, wide] = (jnp.dot(tri[d], gp_ref[0, :, wide], preferred_element_type=F32)
                          + jnp.dot(tri[d], gp_ref[1, :, wide], preferred_element_type=F32)
                          + jnp.dot(tri[d], gp_ref[2, :, wide], preferred_element_type=F32))

    def stage3(slot, t, d):
        wide = slice(d * HGRN_WIDTH, (d + 1) * HGRN_WIDTH)
        hq = ins[d][0][chunk_rows(d, t), :]
        b = b_ref[:, wide]
        if d == 0:
            ref = b[c // 2 - 1:c // 2, :]
            b_last = b[c - 1:c, :]
        else:
            ref = b[c // 2:c // 2 + 1, :]
            b_last = b[0:1, :]
        qd = hq * jnp.exp(b - ref)
        kd = kk_ref[:, wide] * jnp.exp(ref - b)
        staged = (qd, kd, qd * jnp.exp(ref), kd * jnp.exp(b_last - ref))
        for kind, val in enumerate(staged):
            for hd in range(HGRN_HEADS):
                p_ref[slot, d, kind, hd] = val[:, hd * HGRN_DK:(hd + 1) * HGRN_DK].astype(BF16)
        dc_ref[slot, d, 0:1, :] = jnp.exp(b_last)

    units = [(d, hd) for d in range(2) for hd in range(HGRN_HEADS)]

    def step(slot, t, prep_next):
        if prep_next:
            stage1(t + 1, 0)
            stage1(t + 1, 1)
        for d, hd in units:
            a = lax.dot_general(p_ref[slot, d, 0, hd], p_ref[slot, d, 1, hd], nt,
                                preferred_element_type=F32)
            a_ref[d * HGRN_HEADS + hd] = jnp.where(keep[d], a, 0.0).astype(BF16)
        if prep_next:
            cumsum(0)
            cumsum(1)
        for d in range(2):
            v_all = ins[d][2][chunk_rows(d, t), :]
            for hd in range(HGRN_HEADS):
                vh = v_all[:, hd * HGRN_DK:(hd + 1) * HGRN_DK]
                u_ref[d * HGRN_HEADS + hd] = lax.dot_general(vh, p_ref[slot, d, 3, hd], tn,
                                                             preferred_element_type=F32)
        for d in range(2):
            rows = chunk_rows(d, t)
            v_all = ins[d][2][rows, :]
            outs = []
            for hd in range(HGRN_HEADS):
                idx = d * HGRN_HEADS + hd
                o = lax.dot_general(p_ref[slot, d, 2, hd], sbf_ref[idx], nt, preferred_element_type=F32)
                o = o + jnp.dot(a_ref[idx], v_all[:, hd * HGRN_DK:(hd + 1) * HGRN_DK],
                                preferred_element_type=F32)
                outs.append(o.astype(ins[d][3].dtype))
            ins[d][3][rows, :] = jnp.concatenate(outs, axis=1)
        if prep_next:
            stage3(1 - slot, t + 1, 0)
            stage3(1 - slot, t + 1, 1)
        for d, hd in units:
            cols = slice(hd * HGRN_DK, (hd + 1) * HGRN_DK)
            idx = d * HGRN_HEADS + hd
            st = s_ref[idx] * dc_ref[slot, d, 0:1, cols] + u_ref[idx]
            s_ref[idx] = st
            sbf_ref[idx] = st.astype(BF16)

    for d in range(2):
        stage1(0, d)
    for d in range(2):
        cumsum(d)
    for d in range(2):
        stage3(0, 0, d)
    for t in range(n_chunks):
        step(t % 2, t, t + 1 < n_chunks)


def _hgrn(hq, xf_f, xf_b, v, lbl, batch, seq_len):
    tm = HGRN_TILE
    nblk = seq_len // tm
    n = v.shape[0]
    w = HGRN_WIDTH
    fwd = pl.BlockSpec((tm, w), lambda b, i: (b * nblk + i, 0))
    bwd = pl.BlockSpec((tm, w), lambda b, i: (b * nblk + nblk - 1 - i, 0))
    out = jax.ShapeDtypeStruct((n, w), BF16)
    nu = 2 * HGRN_HEADS
    return pl.pallas_call(
        _hgrn_kernel,
        grid=(batch, nblk),
        in_specs=[fwd, fwd, fwd, bwd, bwd, bwd, _const_spec(lbl.shape)],
        out_specs=[fwd, bwd],
        out_shape=[out, out],
        scratch_shapes=[pltpu.VMEM((nu, HGRN_DK, HGRN_DK), F32),
                        pltpu.VMEM((nu, HGRN_DK, HGRN_DK), BF16),
                        pltpu.VMEM((2, 2, 4, HGRN_HEADS, HGRN_CHUNK, HGRN_DK), BF16),
                        pltpu.VMEM((2, 2, SUBLANES, w), F32),
                        pltpu.VMEM((nu, HGRN_CHUNK, HGRN_CHUNK), BF16),
                        pltpu.VMEM((nu, HGRN_DK, HGRN_DK), F32),
                        pltpu.VMEM((3, HGRN_CHUNK, 2 * w), BF16),
                        pltpu.VMEM((HGRN_CHUNK, 2 * w), F32),
                        pltpu.VMEM((HGRN_CHUNK, 2 * w), F32)],
        compiler_params=pltpu.CompilerParams(
            dimension_semantics=("arbitrary", "arbitrary"), vmem_limit_bytes=VMEM_LIMIT_BYTES),
        name="hgrn",
    )(hq, xf_f, v, hq, xf_b, v, lbl)


def _ffn_kernel(xp_ref, xc_ref, xn_ref, nw_ref, wu_ref, cw_ref, cb_ref, wd_ref, fw_ref,
                y_ref, hb_ref, hbf_ref, u_ref, act_ref, *, tiles_per_seq):
    i = pl.program_id(0)
    tm = xc_ref.shape[0]
    hl = SUBLANES
    nw = nw_ref[...]
    pos = i % tiles_per_seq
    keep_prev = jnp.where(pos != 0, 1.0, 0.0)
    keep_next = jnp.where(pos != tiles_per_seq - 1, 1.0, 0.0)
    hb_ref[0:hl, :] = _rms(xp_ref[...], nw) * keep_prev
    hb_ref[hl:hl + tm, :] = _rms(xc_ref[...], nw)
    hb_ref[hl + tm:, :] = _rms(xn_ref[...], nw) * keep_next
    hbf_ref[...] = hb_ref[...].astype(BF16)

    ck = FFN_COL_CHUNK
    nck = D_FF // ck
    nlb = ck // LANES

    def up(j):
        lo, hi = j * ck, (j + 1) * ck
        uv = jnp.dot(hbf_ref[...], wu_ref[:, lo:hi], preferred_element_type=F32)
        ug = jnp.dot(hbf_ref[...], wu_ref[:, D_FF + lo:D_FF + hi], preferred_element_type=F32)
        for k in range(nlb):
            u_ref[j, k] = uv[:, k * LANES:(k + 1) * LANES]
            u_ref[j, nlb + k] = ug[:, k * LANES:(k + 1) * LANES]

    def conv(j, k, col):
        cs = slice(col, col + LANES)
        return (cw_ref[0:1, cs] * u_ref[j, k, hl - 1:hl - 1 + tm, :]
                + cw_ref[1:2, cs] * u_ref[j, k, hl:hl + tm, :]
                + cw_ref[2:3, cs] * u_ref[j, k, hl + 1:hl + 1 + tm, :]
                + cb_ref[:, cs])

    def act(j):
        blocks = []
        for k in range(nlb):
            col = j * ck + k * LANES
            val = conv(j, k, col)
            gate = conv(j, nlb + k, D_FF + col)
            blocks.append((gate * jax.nn.sigmoid(gate) * val).astype(BF16))
        act_ref[:, j * ck:(j + 1) * ck] = jnp.concatenate(blocks, axis=1)

    up(0)
    for j in range(nck):
        if j + 1 < nck:
            up(j + 1)
        act(j)
    y = xc_ref[...] + jnp.dot(act_ref[...], wd_ref[...], preferred_element_type=F32)
    y_ref[...] = _rms(y, fw_ref[...])


def _ffn(x1, nw, wu, cw, cb, wd, fw, seq_len):
    n = x1.shape[0]
    tm = TOKEN_TILE
    hl = SUBLANES
    per = tm // hl
    nhalo = n // hl
    row = lambda i: (i, 0)
    prev = lambda i: (jnp.maximum(i * per - 1, 0), 0)
    nxt = lambda i: (jnp.minimum((i + 1) * per, nhalo - 1), 0)
    kern = functools.partial(_ffn_kernel, tiles_per_seq=seq_len // tm)
    return pl.pallas_call(
        kern,
        grid=(n // tm,),
        in_specs=[pl.BlockSpec((hl, D_MODEL), prev), pl.BlockSpec((tm, D_MODEL), row),
                  pl.BlockSpec((hl, D_MODEL), nxt),
                  _const_spec((1, D_MODEL)), _const_spec((D_MODEL, 2 * D_FF)),
                  _const_spec((3, 2 * D_FF)), _const_spec((1, 2 * D_FF)),
                  _const_spec((D_FF, D_MODEL)), _const_spec((1, D_MODEL))],
        out_specs=pl.BlockSpec((tm, D_MODEL), row),
        out_shape=jax.ShapeDtypeStruct((n, D_MODEL), F32),
        scratch_shapes=[pltpu.VMEM((tm + 2 * hl, D_MODEL), F32),
                        pltpu.VMEM((tm + 2 * hl, D_MODEL), BF16),
                        pltpu.VMEM((D_FF // FFN_COL_CHUNK, 2 * FFN_COL_CHUNK // LANES, tm + 2 * hl, LANES), F32),
                        pltpu.VMEM((tm, D_FF), BF16)],
        compiler_params=pltpu.CompilerParams(
            dimension_semantics=("arbitrary",), vmem_limit_bytes=VMEM_LIMIT_BYTES),
        name="ffn",
    )(x1, x1, x1, nw, wu, cw, cb, wd, fw)


def _rope_tables(seq_len):
    half = HEAD_DIM // 2
    inv_freq = ROPE_THETA ** (-jnp.arange(half, dtype=F32) / half)
    ang = jnp.arange(seq_len, dtype=F32)[:, None] * inv_freq[None, :]
    cos = jnp.cos(ang)
    sin = jnp.sin(ang)
    reps = LANES // HEAD_DIM
    cos_t = jnp.tile(jnp.concatenate([cos, cos], axis=1), (1, reps))
    sin_t = jnp.tile(jnp.concatenate([-sin, sin], axis=1), (1, reps))
    return cos_t, sin_t


def _trunk(x, p):
    batch, seq_len, _ = x.shape
    n = batch * seq_len
    x2 = x.reshape(n, D_MODEL)
    cos, sin = p["rope"][0][:seq_len], p["rope"][1][:seq_len]
    q, k, v, hq, xf_f, xf_b, hi = _inproj(x2, p["norm_mix_w"], p["w_in"], cos, sin, seq_len)
    of, ob = _hgrn(hq, xf_f, xf_b, hi, p["lbl"], batch, seq_len)
    x1 = _mixattn(p["sink"], x2, q, k, v, of, ob, p["norm_mix_w"], p["w_gate"], p["hgrn_norm_w"], p["w_attn_out"],
                  p["w_hgrn_out"], p["w_mix_out"], batch, seq_len)
    y = _ffn(x1, p["norm_ffn_w"], p["w_up"], p["conv_w"], p["conv_b"], p["w_down"], p["norm_final_w"], seq_len)
    return y.reshape(batch, seq_len, D_MODEL)


def kernel(x_prompt, x_sample, norm_mix_w, w_in, attn_sink, hgrn_lb_logits, hgrn_norm_w, w_attn_out, w_hgrn_out, w_mix_out, norm_ffn_w, w_up, conv_w, conv_b, w_down, norm_final_w):
    p = {
        "rope": _rope_tables(max(x_prompt.shape[1], x_sample.shape[1])),
        "norm_mix_w": norm_mix_w[0].reshape(1, D_MODEL),
        "w_in": w_in[0, :, :_OFF_HG].astype(BF16),
        "w_gate": w_in[0, :, _OFF_HG:].astype(BF16),
        "sink": attn_sink[0].astype(F32),
        "lbl": hgrn_lb_logits.astype(F32).reshape(hgrn_lb_logits.shape[0], 2 * HGRN_WIDTH),
        "hgrn_norm_w": hgrn_norm_w[0].astype(F32).reshape(1, HGRN_WIDTH),
        "w_attn_out": w_attn_out[0].astype(BF16),
        "w_hgrn_out": w_hgrn_out[0].astype(BF16),
        "w_mix_out": w_mix_out[0].astype(BF16),
        "norm_ffn_w": norm_ffn_w[0].reshape(1, D_MODEL),
        "w_up": w_up[0].astype(BF16),
        "conv_w": conv_w[0],
        "conv_b": conv_b[0].reshape(1, 2 * D_FF),
        "w_down": w_down[0].astype(BF16),
        "norm_final_w": norm_final_w.reshape(1, D_MODEL),
    }
    return (_trunk(x_prompt, p), _trunk(x_sample, p))
```
